```python
import math
import jax, jax.numpy as jnp
from jax import lax
import numpy as np

D_MODEL = 2048
BATCH = 4
SEQ = 2048
DEPTH = 4
DEC_BATCH = 128
DEC_SEQ = 8
PAST_LEN = 16384
PAGE_SIZE = 128

PLE_DIM = 256
SSM_WIDTH = D_MODEL // 2
SSM_GROUP = 16
SSM_GROUPS = SSM_WIDTH // SSM_GROUP
SSM_STATE = 64
GLA_HEADS = 4
GLA_KEY_WIDTH = D_MODEL // 4
GLA_VAL_WIDTH = D_MODEL // 2
GLA_DK = GLA_KEY_WIDTH // GLA_HEADS
GLA_DV = GLA_VAL_WIDTH // GLA_HEADS
GLA_GATE_RANK = 16
GLA_TAU = 16.0
GLA_CHUNK = 64
CONV_WIDTH = D_MODEL // 2
CONV_K = 31
N_BRANCH = 3
DN_ALPHA = (2 * DEPTH) ** 0.25
DN_BETA = (8 * DEPTH) ** -0.25
LN_EPS = 1e-5

kernel_name = 'hybrid_s5_gla_conformer_gated_step'


def _in_sizes():
    return [SSM_WIDTH, SSM_WIDTH,
            GLA_KEY_WIDTH, GLA_KEY_WIDTH, GLA_VAL_WIDTH,
            GLA_GATE_RANK, GLA_VAL_WIDTH,
            CONV_WIDTH, CONV_WIDTH, CONV_WIDTH,
            N_BRANCH * D_MODEL]


def _split_points():
    return [int(v) for v in np.cumsum(_in_sizes())[:-1]]


def layer_norm(x, g, b):
    xf = x.astype(jnp.float32)
    mu = jnp.mean(xf, axis=-1, keepdims=True)
    var = jnp.mean(jnp.square(xf - mu), axis=-1, keepdims=True)
    return ((xf - mu) * lax.rsqrt(var + LN_EPS) * g.astype(jnp.float32) + b.astype(jnp.float32)).astype(x.dtype)


def s5_mixer(u, h0_re, h0_im, a_re, a_im, log_dt, b_re, b_im, c_re, c_im, d):
    f32 = jnp.float32
    bsz, t, _ = u.shape
    lam = lax.complex(a_re.astype(f32), a_im.astype(f32))
    dt = jnp.exp(log_dt.astype(f32))[:, None]
    a_bar = jnp.exp(lam * dt)
    b_bar = ((a_bar - 1.0) / lam)[..., None] * lax.complex(b_re.astype(f32), b_im.astype(f32))
    ug = u.astype(f32).reshape(bsz, t, SSM_GROUPS, SSM_GROUP)
    bu = jnp.einsum('btgc,gpc->btgp', ug.astype(jnp.complex64), b_bar)
    h0 = lax.complex(h0_re.astype(f32), h0_im.astype(f32))
    bu = bu.at[:, 0].add(a_bar * h0)
    a_seq = jnp.broadcast_to(a_bar, bu.shape)

    def combine(left, right):
        a_l, b_l = left
        a_r, b_r = right
        return a_l * a_r, a_r * b_l + b_r

    _, h = lax.associative_scan(combine, (a_seq, bu), axis=1)
    c = lax.complex(c_re.astype(f32), c_im.astype(f32))
    y = jnp.einsum('btgp,gcp->btgc', h, c).real + d.astype(f32).reshape(SSM_GROUPS, SSM_GROUP) * ug
    h_last = h[:, -1]
    return (y.reshape(bsz, t, SSM_WIDTH).astype(u.dtype),
            h_last.real.astype(h0_re.dtype), h_last.imag.astype(h0_im.dtype))


def gla_mixer(q, k, v, a_low, s0, w_a2, b_a, norm_g):
    f32 = jnp.float32
    bsz, t, _ = q.shape
    out_dtype = v.dtype
    q = q.astype(f32).reshape(bsz, t, GLA_HEADS, GLA_DK) * (GLA_DK ** -0.5)
    k = k.astype(f32).reshape(bsz, t, GLA_HEADS, GLA_DK)
    v = v.astype(f32).reshape(bsz, t, GLA_HEADS, GLA_DV)
    logit = (a_low @ w_a2 + b_a).astype(f32)
    log_a = (jax.nn.log_sigmoid(logit) / GLA_TAU).reshape(bsz, t, GLA_HEADS, GLA_DK)
    c = min(GLA_CHUNK, t)
    tp = -(-t // c) * c
    pad = tp - t
    if pad:
        padf = lambda z: jnp.pad(z, ((0, 0), (0, pad), (0, 0), (0, 0)))
        q, k, v, log_a = padf(q), padf(k), padf(v), padf(log_a)
    n = tp // c
    chunk = lambda z: z.reshape(bsz, n, c, GLA_HEADS, z.shape[-1])
    qc, kc, vc, la = chunk(q), chunk(k), chunk(v), chunk(log_a)
    cum = jnp.cumsum(la, axis=2)
    qd = qc * jnp.exp(cum)
    kd = kc * jnp.exp(-cum)
    mask = jnp.tril(jnp.ones((c, c), dtype=bool))
    att = jnp.where(mask, jnp.einsum('bnchd,bnjhd->bnhcj', qd, kd), 0.0)
    o_intra = jnp.einsum('bnhcj,bnjhv->bnchv', att, vc)
    last = cum[:, :, -1]
    kv = jnp.einsum('bnchd,bnchv->bnhdv', kc * jnp.exp(last[:, :, None] - cum), vc)

    def step(s, xs):
        qd_n, last_n, kv_n = xs
        o = jnp.einsum('bchd,bhdv->bchv', qd_n, s)
        return jnp.exp(last_n)[..., None] * s + kv_n, o

    s_t, o_inter = lax.scan(step, s0.astype(f32),
                            (jnp.moveaxis(qd, 1, 0), jnp.moveaxis(last, 1, 0), jnp.moveaxis(kv, 1, 0)))
    o = (o_intra + jnp.moveaxis(o_inter, 0, 1)).reshape(bsz, tp, GLA_HEADS, GLA_DV)[:, :t]
    mu = jnp.mean(o, axis=-1, keepdims=True)
    var = jnp.mean(jnp.square(o - mu), axis=-1, keepdims=True)
    o = (o - mu) * lax.rsqrt(var + LN_EPS) * norm_g.astype(f32).reshape(GLA_HEADS, GLA_DV)
    return o.reshape(bsz, t, GLA_VAL_WIDTH).astype(out_dtype), s_t.astype(s0.dtype)


def conv_mixer(ga, gb, buf, w_dw, b_dw, ln_g, ln_b):
    g = ga * jax.nn.sigmoid(gb)
    xp = jnp.concatenate([buf.astype(g.dtype), g], axis=1)
    y = lax.conv_general_dilated(xp, w_dw[:, None, :].astype(g.dtype), window_strides=(1,), padding='VALID',
                                 dimension_numbers=('NWC', 'WIO', 'NWC'),
                                 feature_group_count=CONV_WIDTH) + b_dw
    y = jax.nn.silu(layer_norm(y, ln_g, ln_b))
    return y, xp[:, -(CONV_K - 1):].astype(buf.dtype)


def trunk_layer(x, p, s5_re, s5_im, gla_s, conv_buf,
                w_in, b_in, s5_a_re, s5_a_im, s5_log_dt, s5_b_re, s5_b_im, s5_c_re, s5_c_im, s5_d,
                w_glu, b_glu, gla_w_a2, gla_b_a, gla_norm_g, conv_w, conv_b, conv_ln_g, conv_ln_b,
                p_s5, p_gla, p_conv, w_o, w_pg, w_pe, ln_g, ln_b):
    proj = x @ w_in + b_in
    (s5_u, s5_z, q, k, v, a_low, gla_z, ca, cb, cz, gates) = jnp.split(proj, _split_points(), axis=-1)
    y_s5, s5_re_new, s5_im_new = s5_mixer(s5_u, s5_re, s5_im, s5_a_re, s5_a_im, s5_log_dt,
                                          s5_b_re, s5_b_im, s5_c_re, s5_c_im, s5_d)
    y_s5 = jax.nn.gelu(y_s5)
    y_s5 = y_s5 * jax.nn.sigmoid(y_s5 @ w_glu + b_glu)
    br_s5 = (y_s5 * jax.nn.silu(s5_z)) @ p_s5
    o_gla, gla_new = gla_mixer(q, k, v, a_low, gla_s, gla_w_a2, gla_b_a, gla_norm_g)
    br_gla = (o_gla * jax.nn.silu(gla_z)) @ p_gla
    y_c, conv_new = conv_mixer(ca, cb, conv_buf, conv_w, conv_b, conv_ln_g, conv_ln_b)
    br_conv = (y_c * jax.nn.silu(cz)) @ p_conv
    g_s5, g_gla, g_conv = jnp.split(jax.nn.sigmoid(gates), N_BRANCH, axis=-1)
    out = (g_s5 * br_s5 + g_gla * br_gla + g_conv * br_conv) @ w_o
    h = DN_ALPHA * x + out
    h = h + jax.nn.sigmoid(h @ w_pg) * (p @ w_pe)
    return layer_norm(h, ln_g, ln_b), s5_re_new, s5_im_new, gla_new, conv_new


def setup_inputs(seed: int = 0) -> dict:
    key = jax.random.key(seed)
    ks = iter(jax.random.split(key, 40))
    f32 = jnp.float32
    nrm = lambda shape, scale: jax.random.normal(next(ks), shape, f32) * scale
    n_in = sum(_in_sizes())
    G, P, CG = SSM_GROUPS, SSM_STATE, SSM_GROUP
    a_im0 = jnp.pi * jnp.arange(P, dtype=f32)
    return {
        'x_prompt': nrm((BATCH, SEQ, D_MODEL), 1.0),
        'x_sample': nrm((DEC_BATCH, DEC_SEQ, D_MODEL), 1.0),
        'p_prompt': nrm((DEPTH, BATCH, SEQ, PLE_DIM), 1.0),
        'p_sample': nrm((DEPTH, DEC_BATCH, DEC_SEQ, PLE_DIM), 1.0),
        'state_s5_re': nrm((DEPTH, DEC_BATCH, G, P), 0.1),
        'state_s5_im': nrm((DEPTH, DEC_BATCH, G, P), 0.1),
        'state_gla': nrm((DEPTH, DEC_BATCH, GLA_HEADS, GLA_DK, GLA_DV), 0.1),
        'cache_conv': nrm((DEPTH, DEC_BATCH, CONV_K - 1, CONV_WIDTH), 0.5),
        'w_in': nrm((DEPTH, D_MODEL, n_in), D_MODEL ** -0.5),
        'b_in': nrm((DEPTH, n_in), 0.01),
        's5_a_re': -0.5 + nrm((DEPTH, G, P), 0.01),
        's5_a_im': a_im0 + nrm((DEPTH, G, P), 0.01),
        's5_log_dt': jax.random.uniform(next(ks), (DEPTH, G), f32, math.log(1e-3), math.log(1e-1)),
        's5_b_re': nrm((DEPTH, G, P, CG), (2 * CG) ** -0.5),
        's5_b_im': nrm((DEPTH, G, P, CG), (2 * CG) ** -0.5),
        's5_c_re': nrm((DEPTH, G, CG, P), P ** -0.5),
        's5_c_im': nrm((DEPTH, G, CG, P), P ** -0.5),
        's5_d': nrm((DEPTH, SSM_WIDTH), 1.0),
        'w_glu': nrm((DEPTH, SSM_WIDTH, SSM_WIDTH), SSM_WIDTH ** -0.5),
        'b_glu': nrm((DEPTH, SSM_WIDTH), 0.01),
        'gla_w_a2': nrm((DEPTH, GLA_GATE_RANK, GLA_KEY_WIDTH), GLA_GATE_RANK ** -0.5),
        'gla_b_a': nrm((DEPTH, GLA_KEY_WIDTH), 0.01),
        'gla_norm_g': 1.0 + nrm((DEPTH, GLA_VAL_WIDTH), 0.02),
        'conv_w': nrm((DEPTH, CONV_K, CONV_WIDTH), CONV_K ** -0.5),
        'conv_b': nrm((DEPTH, CONV_WIDTH), 0.01),
        'conv_ln_g': 1.0 + nrm((DEPTH, CONV_WIDTH), 0.02),
        'conv_ln_b': nrm((DEPTH, CONV_WIDTH), 0.01),
        'p_s5': nrm((DEPTH, SSM_WIDTH, D_MODEL), DN_BETA * SSM_WIDTH ** -0.5),
        'p_gla': nrm((DEPTH, GLA_VAL_WIDTH, D_MODEL), DN_BETA * GLA_VAL_WIDTH ** -0.5),
        'p_conv': nrm((DEPTH, CONV_WIDTH, D_MODEL), DN_BETA * CONV_WIDTH ** -0.5),
        'w_o': nrm((DEPTH, D_MODEL, D_MODEL), DN_BETA * D_MODEL ** -0.5),
        'w_pg': nrm((DEPTH, D_MODEL, D_MODEL), D_MODEL ** -0.5),
        'w_pe': nrm((DEPTH, PLE_DIM, D_MODEL), DN_BETA * PLE_DIM ** -0.5),
        'ln_g': 1.0 + nrm((DEPTH, D_MODEL), 0.02),
        'ln_b': nrm((DEPTH, D_MODEL), 0.01),
    }


def reference(x_prompt, x_sample, p_prompt, p_sample, state_s5_re, state_s5_im, state_gla, cache_conv,
              w_in, b_in, s5_a_re, s5_a_im, s5_log_dt, s5_b_re, s5_b_im, s5_c_re, s5_c_im, s5_d,
              w_glu, b_glu, gla_w_a2, gla_b_a, gla_norm_g, conv_w, conv_b, conv_ln_g, conv_ln_b,
              p_s5, p_gla, p_conv, w_o, w_pg, w_pe, ln_g, ln_b):
    bp = x_prompt.shape[0]
    sd = state_s5_re.dtype
    zero_s5 = jnp.zeros((bp, SSM_GROUPS, SSM_STATE), sd)
    zero_gla = jnp.zeros((bp, GLA_HEADS, GLA_DK, GLA_DV), state_gla.dtype)
    zero_conv = jnp.zeros((bp, CONV_K - 1, CONV_WIDTH), cache_conv.dtype)
    hp, hs = x_prompt, x_sample
    pr_re, pr_im, pr_gla, pr_conv = [], [], [], []
    sm_re, sm_im, sm_gla, sm_conv = [], [], [], []
    for i in range(DEPTH):
        wts = (w_in[i], b_in[i], s5_a_re[i], s5_a_im[i], s5_log_dt[i], s5_b_re[i], s5_b_im[i],
               s5_c_re[i], s5_c_im[i], s5_d[i], w_glu[i], b_glu[i], gla_w_a2[i], gla_b_a[i],
               gla_norm_g[i], conv_w[i], conv_b[i], conv_ln_g[i], conv_ln_b[i],
               p_s5[i], p_gla[i], p_conv[i], w_o[i], w_pg[i], w_pe[i], ln_g[i], ln_b[i])
        hp, a, b, c, d = trunk_layer(hp, p_prompt[i], zero_s5, zero_s5, zero_gla, zero_conv, *wts)
        pr_re.append(a); pr_im.append(b); pr_gla.append(c); pr_conv.append(d)
        hs, a, b, c, d = trunk_layer(hs, p_sample[i], state_s5_re[i], state_s5_im[i], state_gla[i],
                                     cache_conv[i], *wts)
        sm_re.append(a); sm_im.append(b); sm_gla.append(c); sm_conv.append(d)
    return (hp, hs,
            jnp.stack(pr_re), jnp.stack(pr_im), jnp.stack(pr_gla), jnp.stack(pr_conv),
            jnp.stack(sm_re), jnp.stack(sm_im), jnp.stack(sm_gla), jnp.stack(sm_conv))
```

```python
import functools

import numpy as np
import jax
import jax.numpy as jnp
from jax import lax
from jax.experimental import pallas as pl
from jax.experimental.pallas import tpu as pltpu

F32 = jnp.float32
BF16 = jnp.bfloat16

D_MODEL = 2048
PLE_DIM = 256
SSM_WIDTH = 1024
SSM_GROUP = 16
SSM_GROUPS = 64
SSM_STATE = 64
N_STATE = SSM_GROUPS * SSM_STATE
GLA_HEADS = 4
GLA_DK = 128
GLA_DV = 256
GLA_KEY_WIDTH = 512
GLA_VAL_WIDTH = 1024
GLA_GATE_RANK = 16
GLA_TAU = 16.0
CONV_WIDTH = 1024
CONV_K = 31
CONV_HIST = 32
N_BRANCH = 3
DN_ALPHA = 8.0 ** 0.25
LN_EPS = 1e-5

CHUNK = 64
SEQ_ROWS = 256
S5_SLAB = 1024
S5_USLAB = 256
N_SLAB = N_STATE // S5_SLAB

VMEM_LIMIT = 56 * 1024 * 1024

PROJ_BN = 512
COL_GATES = 0
COL_S5_U = 6144
COL_V = 7168
COL_CA = 8192
COL_S5_Z = 9216
COL_GLA_Z = 10240
COL_CZ = 11264
COL_CB = 12288
COL_Q = 13312
COL_K = 13824
COL_ALOW = 14336
ALOW_PAD = 128
PROJ_COLS = 14848


def _const_spec(shape):
    zeros = (0,) * len(shape)
    return pl.BlockSpec(shape, lambda *_: zeros, pipeline_mode=pl.Buffered(1))


def _params(sem):
    return pltpu.CompilerParams(dimension_semantics=sem, vmem_limit_bytes=VMEM_LIMIT)


def _inproj_body(x_ref, w_ref, b_ref, o_ref):
    acc = jnp.dot(x_ref[...], w_ref[...], preferred_element_type=F32) + b_ref[...]
    j = pl.program_id(1)
    is_sig = (j < COL_S5_U // PROJ_BN) | ((j >= COL_CB // PROJ_BN) & (j < COL_Q // PROJ_BN))
    is_silu = (j >= COL_S5_Z // PROJ_BN) & (j < COL_CB // PROJ_BN)

    @pl.when(is_sig)
    def _():
        o_ref[...] = jax.nn.sigmoid(acc).astype(o_ref.dtype)

    @pl.when(is_silu)
    def _():
        o_ref[...] = jax.nn.silu(acc).astype(o_ref.dtype)

    @pl.when(jnp.logical_not(is_sig | is_silu))
    def _():
        o_ref[...] = acc.astype(o_ref.dtype)


def _inproj(x, w, b):
    m = x.shape[0]
    bm = min(m, 1024)
    return pl.pallas_call(
        _inproj_body,
        grid=(m // bm, PROJ_COLS // PROJ_BN),
        in_specs=[pl.BlockSpec((bm, D_MODEL), lambda i, j: (i, 0)),
                  pl.BlockSpec((D_MODEL, PROJ_BN), lambda i, j: (0, j)),
                  pl.BlockSpec((1, PROJ_BN), lambda i, j: (0, j))],
        out_specs=pl.BlockSpec((bm, PROJ_BN), lambda i, j: (i, j)),
        out_shape=jax.ShapeDtypeStruct((m, PROJ_COLS), BF16),
        compiler_params=_params(("arbitrary", "arbitrary")),
        name="inproj",
    )(x, w, b)


def _s5_prep_body(ar_ref, ai_ref, ldt_ref, bre_ref, bim_ref, apr_ref, api_ref, anr_ref, ani_ref, bbr_ref, bbi_ref):
    ar = ar_ref[0]
    ai = ai_ref[0]
    dt = jnp.exp(ldt_ref[0])
    xr = ar * dt
    xi = ai * dt
    t = lax.broadcasted_iota(jnp.int32, (CHUNK, N_STATE), 0).astype(F32)
    ang = t * xi
    cs = jnp.cos(ang)
    sn = jnp.sin(ang)
    mag = jnp.exp(t * xr)
    imag = jnp.exp(-(t * xr))
    apr_ref[0] = mag * cs
    api_ref[0] = mag * sn
    anr_ref[0] = imag * cs
    ani_ref[0] = -(imag * sn)
    e1 = jnp.exp(xr)
    nr = e1 * jnp.cos(xi) - 1.0
    ni = e1 * jnp.sin(xi)
    den = ar * ar + ai * ai
    cr = (nr * ar + ni * ai) / den
    ci = (ni * ar - nr * ai) / den
    bre = bre_ref[0]
    bim = bim_ref[0]
    bbr_ref[0] = cr * bre - ci * bim
    bbi_ref[0] = cr * bim + ci * bre


def _s5_prep(a_re, a_im, log_dt, b_re, b_im):
    depth = a_re.shape[0]
    ar = a_re.reshape(depth, 1, N_STATE)
    ai = a_im.reshape(depth, 1, N_STATE)
    ldt = jnp.broadcast_to(log_dt[:, :, None], (depth, SSM_GROUPS, SSM_STATE)).reshape(depth, 1, N_STATE)
    bre = jnp.transpose(b_re, (0, 3, 1, 2)).reshape(depth, SSM_GROUP, N_STATE)
    bim = jnp.transpose(b_im, (0, 3, 1, 2)).reshape(depth, SSM_GROUP, N_STATE)
    row = pl.BlockSpec((1, 1, N_STATE), lambda i: (i, 0, 0))
    grp = pl.BlockSpec((1, SSM_GROUP, N_STATE), lambda i: (i, 0, 0))
    tab = pl.BlockSpec((1, CHUNK, N_STATE), lambda i: (i, 0, 0))
    tab_shape = jax.ShapeDtypeStruct((depth, CHUNK, N_STATE), F32)
    grp_shape = jax.ShapeDtypeStruct((depth, SSM_GROUP, N_STATE), F32)
    return pl.pallas_call(
        _s5_prep_body,
        grid=(depth,),
        in_specs=[row, row, row, grp, grp],
        out_specs=[tab, tab, tab, tab, grp, grp],
        out_shape=[tab_shape, tab_shape, tab_shape, tab_shape, grp_shape, grp_shape],
        compiler_params=_params(("arbitrary",)),
        name="s5_prep",
    )(ar, ai, ldt, bre, bim)


def _s5_block_weights(bbar, c):
    gps = SSM_GROUPS // N_SLAB
    eye = jnp.eye(gps, dtype=F32)
    b4 = bbar.reshape(SSM_GROUP, N_SLAB, gps, SSM_STATE)
    b_blk = jnp.einsum('inlp,lm->nlimp', b4, eye)
    b_blk = b_blk.reshape(N_SLAB, S5_USLAB, S5_SLAB)
    c4 = c.reshape(N_SLAB, gps, SSM_GROUP, SSM_STATE)
    c_blk = jnp.einsum('nlip,lm->nlpmi', c4, eye)
    c_blk = c_blk.reshape(N_SLAB, S5_SLAB, S5_USLAB)
    return b_blk.astype(BF16), c_blk.astype(BF16)


def _cmul(ar, ai, br, bi):
    return ar * br - ai * bi, ar * bi + ai * br


def _s5_body(*refs, lt, nsub, prompt):
    if prompt:
        (u_ref, sz_ref, bre_ref, bim_ref, cre_ref, cim_ref, tri_ref, apr_ref, api_ref, anr_ref, ani_ref,
         d_ref, wglu_ref, bglu_ref, o_ref, hre_ref, him_ref, y_scr) = refs
        pre_ref, pim_ref = hre_ref, him_ref

        @pl.when(pl.program_id(0) == 0)
        def _():
            hre_ref[...] = jnp.zeros_like(hre_ref)
            him_ref[...] = jnp.zeros_like(him_ref)
    else:
        (u_ref, sz_ref, bre_ref, bim_ref, cre_ref, cim_ref, tri_ref, apr_ref, api_ref, anr_ref, ani_ref,
         d_ref, wglu_ref, bglu_ref, pre_ref, pim_ref, o_ref, hre_ref, him_ref, y_scr) = refs
    rows = nsub * lt
    tri = tri_ref[...]
    for nt in range(N_SLAB):
        sl = slice(nt * S5_SLAB, (nt + 1) * S5_SLAB)
        u_nt = u_ref[:, nt * S5_USLAB:(nt + 1) * S5_USLAB]
        bu_re = jnp.dot(u_nt, bre_ref[nt], preferred_element_type=F32).reshape(nsub, lt, S5_SLAB)
        bu_im = jnp.dot(u_nt, bim_ref[nt], preferred_element_type=F32).reshape(nsub, lt, S5_SLAB)
        w_re, w_im = _cmul(bu_re, bu_im, anr_ref[:, sl][None], ani_ref[:, sl][None])
        w_re = w_re.reshape(rows, S5_SLAB).astype(BF16)
        w_im = w_im.reshape(rows, S5_SLAB).astype(BF16)
        acc_re = jnp.dot(tri, w_re, preferred_element_type=F32).reshape(nsub, lt, S5_SLAB)
        acc_im = jnp.dot(tri, w_im, preferred_element_type=F32).reshape(nsub, lt, S5_SLAB)
        c_re, c_im = _cmul(apr_ref[1:2, sl], api_ref[1:2, sl], pre_ref[:, sl], pim_ref[:, sl])
        acc_re = acc_re + c_re[:, None, :]
        acc_im = acc_im + c_im[:, None, :]
        h_re, h_im = _cmul(acc_re, acc_im, apr_ref[:, sl][None], api_ref[:, sl][None])
        hre_ref[:, sl] = h_re[:, lt - 1, :]
        him_ref[:, sl] = h_im[:, lt - 1, :]
        h_re = h_re.reshape(rows, S5_SLAB).astype(BF16)
        h_im = h_im.reshape(rows, S5_SLAB).astype(BF16)
        y_scr[:, nt * S5_USLAB:(nt + 1) * S5_USLAB] = (
            jnp.dot(h_re, cre_ref[nt], preferred_element_type=F32)
            - jnp.dot(h_im, cim_ref[nt], preferred_element_type=F32))
    y = y_scr[...] + d_ref[...] * u_ref[...].astype(F32)
    y = jax.nn.gelu(y)
    gate = jnp.dot(y.astype(BF16), wglu_ref[...], preferred_element_type=F32) + bglu_ref[...]
    y = y * jax.nn.sigmoid(gate)
    o_ref[...] = (y * sz_ref[...].astype(F32)).astype(o_ref.dtype)


def _s5_branch(proj, lw, h0_re, h0_im, *, prompt):
    m = proj.shape[0]
    lt = CHUNK if prompt else m // h0_re.shape[0]
    nsub = SEQ_ROWS // lt
    steps = m // SEQ_ROWS
    seqs = nsub if prompt else m // lt
    tri = _block_tri(SEQ_ROWS, lt)
    col = lambda off, w: pl.BlockSpec((SEQ_ROWS, w), lambda c: (c, off // w))
    tab = pl.BlockSpec((lt, N_STATE), lambda c: (0, 0), pipeline_mode=pl.Buffered(1))
    in_specs = [col(COL_S5_U, SSM_WIDTH), col(COL_S5_Z, SSM_WIDTH),
                _const_spec((N_SLAB, S5_USLAB, S5_SLAB)), _const_spec((N_SLAB, S5_USLAB, S5_SLAB)),
                _const_spec((N_SLAB, S5_SLAB, S5_USLAB)), _const_spec((N_SLAB, S5_SLAB, S5_USLAB)),
                _const_spec((SEQ_ROWS, SEQ_ROWS)), tab, tab, tab, tab,
                _const_spec((1, SSM_WIDTH)), _const_spec((SSM_WIDTH, SSM_WIDTH)), _const_spec((1, SSM_WIDTH))]
    args = [proj, proj, lw['s5_bre'], lw['s5_bim'], lw['s5_cre'], lw['s5_cim'], tri,
            lw['s5_apr'], lw['s5_api'], lw['s5_anr'], lw['s5_ani'], lw['s5_d'], lw['w_glu'], lw['b_glu']]
    if prompt:
        state_spec = pl.BlockSpec((nsub, N_STATE), lambda c: (0, 0))
    else:
        state_spec = pl.BlockSpec((nsub, N_STATE), lambda c: (c, 0))
        in_specs += [state_spec, state_spec]
        args += [h0_re, h0_im]
    state_shape = jax.ShapeDtypeStruct((seqs, N_STATE), F32)
    return pl.pallas_call(
        functools.partial(_s5_body, lt=lt, nsub=nsub, prompt=prompt),
        grid=(steps,),
        in_specs=in_specs,
        out_specs=[pl.BlockSpec((SEQ_ROWS, SSM_WIDTH), lambda c: (c, 0)), state_spec, state_spec],
        out_shape=[jax.ShapeDtypeStruct((m, SSM_WIDTH), BF16), state_shape, state_shape],
        scratch_shapes=[pltpu.VMEM((SEQ_ROWS, SSM_WIDTH), F32)],
        compiler_params=_params(("arbitrary",)),
        name="s5_prompt" if prompt else "s5_sample",
    )(*args)


def _block_tri(rows, lt):
    r = np.arange(rows)
    same = (r[:, None] // lt) == (r[None, :] // lt)
    return jnp.asarray(np.where(same & (r[:, None] >= r[None, :]), 1.0, 0.0), dtype=BF16)


def _gla_body(*refs, lt, nsub, prompt):
    if prompt:
        (q_ref, k_ref, v_ref, al_ref, gz_ref, wa2_ref, ba_ref, ng_ref, tri_ref, o_ref, s_ref) = refs
        s0_ref = s_ref

        @pl.when(pl.program_id(0) == 0)
        def _():
            s_ref[...] = jnp.zeros_like(s_ref)
    else:
        (q_ref, k_ref, v_ref, al_ref, gz_ref, wa2_ref, ba_ref, ng_ref, tri_ref, s0_ref, o_ref, s_ref) = refs
    rows = nsub * lt
    tri = tri_ref[...]
    logit = jnp.dot(al_ref[...], wa2_ref[...], preferred_element_type=F32) + ba_ref[...]
    log_a = jax.nn.log_sigmoid(logit) * (1.0 / GLA_TAU)
    hi = log_a.astype(BF16)
    lo = (log_a - hi.astype(F32)).astype(BF16)
    cum = (jnp.dot(tri, hi, preferred_element_type=F32) + jnp.dot(tri, lo, preferred_element_type=F32))
    q = q_ref[...].astype(F32) * (GLA_DK ** -0.5)
    k = k_ref[...].astype(F32)
    qd = (q * jnp.exp(cum)).astype(BF16)
    kd = (k * jnp.exp(-cum)).astype(BF16)
    cum3 = cum.reshape(nsub, lt, GLA_KEY_WIDTH)
    last = cum3[:, lt - 1:lt, :]
    kk = (k.reshape(nsub, lt, GLA_KEY_WIDTH) * jnp.exp(last - cum3)).astype(BF16)
    e_last = jnp.exp(last)
    ri = lax.broadcasted_iota(jnp.int32, (lt, lt), 0)
    ci = lax.broadcasted_iota(jnp.int32, (lt, lt), 1)
    causal = ri >= ci
    for s in range(nsub):
        rs = slice(s * lt, (s + 1) * lt)
        for h in range(GLA_HEADS):
            ks = slice(h * GLA_DK, (h + 1) * GLA_DK)
            vs = slice(h * GLA_DV, (h + 1) * GLA_DV)
            qd_sh = qd[rs, ks]
            v_sh = v_ref[rs, vs]
            att = lax.dot_general(qd_sh, kd[rs, ks], (((1,), (1,)), ((), ())), preferred_element_type=F32)
            att = jnp.where(causal, att, 0.0).astype(BF16)
            state = s0_ref[s, h]
            o = (jnp.dot(att, v_sh, preferred_element_type=F32)
                 + jnp.dot(qd_sh, state.astype(BF16), preferred_element_type=F32))
            kv = lax.dot_general(kk[s, :, ks], v_sh, (((0,), (0,)), ((), ())), preferred_element_type=F32)
            dec = jnp.transpose(jnp.broadcast_to(e_last[s, :, ks], (GLA_DK, GLA_DK)))
            s_ref[s, h] = jnp.concatenate([dec, dec], axis=1) * state + kv
            mu = jnp.mean(o, axis=-1, keepdims=True)
            var = jnp.mean(jnp.square(o - mu), axis=-1, keepdims=True)
            o = (o - mu) * lax.rsqrt(var + LN_EPS) * ng_ref[:, vs]
            o_ref[rs, vs] = (o * gz_ref[rs, vs].astype(F32)).astype(o_ref.dtype)


def _gla_branch(proj, lw, s0, *, prompt):
    m = proj.shape[0]
    if prompt:
        lt, rows = CHUNK, SEQ_ROWS
    else:
        lt, rows = m // s0.shape[0], 64
    nsub = rows // lt
    steps = m // rows
    seqs = nsub if prompt else m // lt
    tri = _block_tri(rows, lt)
    col = lambda off, w: pl.BlockSpec((rows, w), lambda c: (c, off // w))
    in_specs = [col(COL_Q, GLA_KEY_WIDTH), col(COL_K, GLA_KEY_WIDTH), col(COL_V, GLA_VAL_WIDTH),
                col(COL_ALOW, ALOW_PAD), col(COL_GLA_Z, GLA_VAL_WIDTH),
                _const_spec((ALOW_PAD, GLA_KEY_WIDTH)), _const_spec((1, GLA_KEY_WIDTH)),
                _const_spec((1, GLA_VAL_WIDTH)), _const_spec((rows, rows))]
    args = [proj, proj, proj, proj, proj, lw['gla_w_a2'], lw['gla_b_a'], lw['gla_norm_g'], tri]
    if prompt:
        state_spec = pl.BlockSpec((nsub, GLA_HEADS, GLA_DK, GLA_DV), lambda c: (0, 0, 0, 0))
    else:
        state_spec = pl.BlockSpec((nsub, GLA_HEADS, GLA_DK, GLA_DV), lambda c: (c, 0, 0, 0))
        in_specs += [state_spec]
        args += [s0]
    return pl.pallas_call(
        functools.partial(_gla_body, lt=lt, nsub=nsub, prompt=prompt),
        grid=(steps,),
        in_specs=in_specs,
        out_specs=[pl.BlockSpec((rows, GLA_VAL_WIDTH), lambda c: (c, 0)), state_spec],
        out_shape=[jax.ShapeDtypeStruct((m, GLA_VAL_WIDTH), BF16),
                   jax.ShapeDtypeStruct((seqs, GLA_HEADS, GLA_DK, GLA_DV), F32)],
        compiler_params=_params(("arbitrary",)),
        name="gla_prompt" if prompt else "gla_sample",
    )(*args)


def _conv_body(*refs, lt, nsub, prompt):
    if prompt:
        (ca_ref, scb_ref, cz_ref, w_ref, b_ref, lng_ref, lnb_ref, o_ref, cache_out_ref, x_scr, y_scr) = refs

        @pl.when(pl.program_id(0) == 0)
        def _():
            x_scr[:, 0:CONV_HIST, :] = jnp.zeros((nsub, CONV_HIST, CONV_WIDTH), F32)
    else:
        (ca_ref, scb_ref, cz_ref, w_ref, b_ref, lng_ref, lnb_ref, cache_ref,
         o_ref, cache_out_ref, x_scr, y_scr) = refs
    pad = CONV_HIST - (CONV_K - 1)
    cw = 256 if lt > 8 else CONV_WIDTH

    def per_seq(s, carry):
        r0 = pl.multiple_of(s * lt, 8)
        g = ca_ref[pl.ds(r0, lt), :].astype(F32) * scb_ref[pl.ds(r0, lt), :].astype(F32)
        x_scr[s, CONV_HIST:CONV_HIST + lt, :] = g
        if not prompt:
            x_scr[s, pad:CONV_HIST, :] = cache_ref[s]
        for c0 in range(0, CONV_WIDTH, cw):
            acc = jnp.zeros((lt, cw), F32)
            for j in range(CONV_K):
                acc = acc + w_ref[j:j + 1, c0:c0 + cw] * x_scr[s, pl.ds(j + pad, lt), c0:c0 + cw]
            y_scr[pl.ds(r0, lt), c0:c0 + cw] = acc
        cache_out_ref[s] = x_scr[s, lt + pad:lt + CONV_HIST, :]
        if prompt:
            x_scr[s, 0:CONV_HIST, :] = x_scr[s, lt:lt + CONV_HIST, :]
        return carry

    lax.fori_loop(0, nsub, per_seq, 0)
    y = y_scr[...] + b_ref[...]
    mu = jnp.mean(y, axis=-1, keepdims=True)
    var = jnp.mean(jnp.square(y - mu), axis=-1, keepdims=True)
    y = (y - mu) * lax.rsqrt(var + LN_EPS) * lng_ref[...] + lnb_ref[...]
    o_ref[...] = (jax.nn.silu(y) * cz_ref[...].astype(F32)).astype(o_ref.dtype)


def _conv_branch(proj, lw, cache, *, prompt):
    m = proj.shape[0]
    lt = CHUNK if prompt else m // cache.shape[0]
    nsub = SEQ_ROWS // lt
    steps = m // SEQ_ROWS
    seqs = nsub if prompt else m // lt
    col = lambda off: pl.BlockSpec((SEQ_ROWS, CONV_WIDTH), lambda c: (c, off // CONV_WIDTH))
    in_specs = [col(COL_CA), col(COL_CB), col(COL_CZ),
                _const_spec((CONV_K, CONV_WIDTH)), _const_spec((1, CONV_WIDTH)),
                _const_spec((1, CONV_WIDTH)), _const_spec((1, CONV_WIDTH))]
    args = [proj, proj, proj, lw['conv_w'], lw['conv_b'], lw['conv_ln_g'], lw['conv_ln_b']]
    if prompt:
        cache_spec = pl.BlockSpec((nsub, CONV_K - 1, CONV_WIDTH), lambda c: (0, 0, 0))
    else:
        cache_spec = pl.BlockSpec((nsub, CONV_K - 1, CONV_WIDTH), lambda c: (c, 0, 0))
        in_specs += [cache_spec]
        args += [cache]
    return pl.pallas_call(
        functools.partial(_conv_body, lt=lt, nsub=nsub, prompt=prompt),
        grid=(steps,),
        in_specs=in_specs,
        out_specs=[pl.BlockSpec((SEQ_ROWS, CONV_WIDTH), lambda c: (c, 0)), cache_spec],
        out_shape=[jax.ShapeDtypeStruct((m, CONV_WIDTH), BF16),
                   jax.ShapeDtypeStruct((seqs, CONV_K - 1, CONV_WIDTH), F32)],
        scratch_shapes=[pltpu.VMEM((nsub, CONV_HIST + lt, CONV_WIDTH), F32),
                        pltpu.VMEM((SEQ_ROWS, CONV_WIDTH), F32)],
        compiler_params=_params(("arbitrary",)),
        name="conv_prompt" if prompt else "conv_sample",
    )(*args)


MERGE_ROWS = 256


def _merge_body(a1_ref, a2_ref, a3_ref, g1_ref, g2_ref, g3_ref, p1_ref, p2_ref, p3_ref, o_ref):
    m = g1_ref[...].astype(F32) * jnp.dot(a1_ref[...], p1_ref[...], preferred_element_type=F32)
    m = m + g2_ref[...].astype(F32) * jnp.dot(a2_ref[...], p2_ref[...], preferred_element_type=F32)
    m = m + g3_ref[...].astype(F32) * jnp.dot(a3_ref[...], p3_ref[...], preferred_element_type=F32)
    o_ref[...] = m.astype(o_ref.dtype)


def _merge(a_s5, a_gla, a_conv, proj, lw):
    m = a_s5.shape[0]
    act = pl.BlockSpec((MERGE_ROWS, SSM_WIDTH), lambda i: (i, 0))
    gate = lambda n: pl.BlockSpec((MERGE_ROWS, D_MODEL), lambda i: (i, COL_GATES // D_MODEL + n))
    wspec = _const_spec((SSM_WIDTH, D_MODEL))
    return pl.pallas_call(
        _merge_body,
        grid=(m // MERGE_ROWS,),
        in_specs=[act, act, act, gate(0), gate(1), gate(2), wspec, wspec, wspec],
        out_specs=pl.BlockSpec((MERGE_ROWS, D_MODEL), lambda i: (i, 0)),
        out_shape=jax.ShapeDtypeStruct((m, D_MODEL), BF16),
        compiler_params=_params(("arbitrary",)),
        name="merge",
    )(a_s5, a_gla, a_conv, proj, proj, proj, lw['p_s5'], lw['p_gla'], lw['p_conv'])


def _post_body(m_ref, x_ref, p_ref, wo_ref, wpg_ref, wpe_ref, g_ref, b_ref, xo_ref, xb_ref):
    h = DN_ALPHA * x_ref[...] + jnp.dot(m_ref[...], wo_ref[...], preferred_element_type=F32)
    gate = jax.nn.sigmoid(jnp.dot(h.astype(BF16), wpg_ref[...], preferred_element_type=F32))
    h = h + gate * jnp.dot(p_ref[...], wpe_ref[...], preferred_element_type=F32)
    mu = jnp.mean(h, axis=-1, keepdims=True)
    var = jnp.mean(jnp.square(h - mu), axis=-1, keepdims=True)
    xn = (h - mu) * lax.rsqrt(var + LN_EPS) * g_ref[...] + b_ref[...]
    xo_ref[...] = xn
    xb_ref[...] = xn.astype(BF16)


def _post(merged, x, p, lw):
    m = x.shape[0]
    row = lambda w: pl.BlockSpec((MERGE_ROWS, w), lambda i: (i, 0))
    return pl.pallas_call(
        _post_body,
        grid=(m // MERGE_ROWS,),
        in_specs=[row(D_MODEL), row(D_MODEL), row(PLE_DIM),
                  _const_spec((D_MODEL, D_MODEL)), _const_spec((D_MODEL, D_MODEL)), _const_spec((PLE_DIM, D_MODEL)),
                  _const_spec((1, D_MODEL)), _const_spec((1, D_MODEL))],
        out_specs=[row(D_MODEL), row(D_MODEL)],
        out_shape=[jax.ShapeDtypeStruct((m, D_MODEL), F32), jax.ShapeDtypeStruct((m, D_MODEL), BF16)],
        compiler_params=_params(("arbitrary",)),
        name="post",
    )(merged, x, p, lw['w_o'], lw['w_pg'], lw['w_pe'], lw['ln_g'], lw['ln_b'])


def _fused_in_weights(w_in, b_in):
    sizes = [SSM_WIDTH, SSM_WIDTH, GLA_KEY_WIDTH, GLA_KEY_WIDTH, GLA_VAL_WIDTH, GLA_GATE_RANK, GLA_VAL_WIDTH,
             CONV_WIDTH, CONV_WIDTH, CONV_WIDTH, N_BRANCH * D_MODEL]
    offs = np.concatenate([[0], np.cumsum(sizes)])
    names = ['s5_u', 's5_z', 'q', 'k', 'v', 'alow', 'gla_z', 'ca', 'cb', 'cz', 'gates']
    src = {n: (int(offs[i]), int(offs[i + 1])) for i, n in enumerate(names)}
    order = ['gates', 's5_u', 'v', 'ca', 's5_z', 'gla_z', 'cz', 'cb', 'q', 'k', 'alow']

    def gather(a):
        parts = [a[..., src[n][0]:src[n][1]] for n in order]
        used = sum(p.shape[-1] for p in parts)
        parts.append(jnp.zeros(a.shape[:-1] + (PROJ_COLS - used,), a.dtype))
        return jnp.concatenate(parts, axis=-1)

    return gather(w_in).astype(BF16), gather(b_in)[:, None, :]


def _layer_weights(i, fused_w, fused_b, prep, s5_c_re, s5_c_im, s5_d, w_glu, b_glu, gla_w_a2, gla_b_a, gla_norm_g,
                   conv_w, conv_b, conv_ln_g, conv_ln_b, p_s5, p_gla, p_conv, w_o, w_pg, w_pe, ln_g, ln_b):
    apr, api, anr, ani, bbr, bbi = prep
    bre, cre = _s5_block_weights(bbr[i], s5_c_re[i])
    bim, cim = _s5_block_weights(bbi[i], s5_c_im[i])
    row = lambda a: a[i][None, :]
    wa2 = jnp.zeros((ALOW_PAD, GLA_KEY_WIDTH), BF16).at[:GLA_GATE_RANK].set(gla_w_a2[i].astype(BF16))
    return dict(
        w_in=fused_w[i], b_in=fused_b[i],
        s5_bre=bre, s5_bim=bim, s5_cre=cre, s5_cim=cim,
        s5_apr=apr[i], s5_api=api[i], s5_anr=anr[i], s5_ani=ani[i],
        s5_d=row(s5_d), w_glu=w_glu[i].astype(BF16), b_glu=row(b_glu),
        gla_w_a2=wa2, gla_b_a=row(gla_b_a), gla_norm_g=row(gla_norm_g),
        conv_w=conv_w[i], conv_b=row(conv_b), conv_ln_g=row(conv_ln_g), conv_ln_b=row(conv_ln_b),
        p_s5=p_s5[i].astype(BF16), p_gla=p_gla[i].astype(BF16), p_conv=p_conv[i].astype(BF16),
        w_o=w_o[i].astype(BF16), w_pg=w_pg[i].astype(BF16), w_pe=w_pe[i].astype(BF16),
        ln_g=row(ln_g), ln_b=row(ln_b))


def _layer(x, xb, p, lw, s5_re, s5_im, gla_s, conv_buf, *, prompt):
    proj = _inproj(xb, lw['w_in'], lw['b_in'])
    a_s5, h_re, h_im = _s5_branch(proj, lw, s5_re, s5_im, prompt=prompt)
    a_gla, gla_new = _gla_branch(proj, lw, gla_s, prompt=prompt)
    a_conv, conv_new = _conv_branch(proj, lw, conv_buf, prompt=prompt)
    merged = _merge(a_s5, a_gla, a_conv, proj, lw)
    x_new, xb_new = _post(merged, x, p, lw)
    return x_new, xb_new, h_re, h_im, gla_new, conv_new


def _to_chunk_major(a, bsz, seq):
    w = a.shape[-1]
    a = a.reshape(bsz, seq // CHUNK, CHUNK, w)
    return jnp.transpose(a, (1, 0, 2, 3)).reshape(bsz * seq, w)


def _from_chunk_major(a, bsz, seq):
    w = a.shape[-1]
    a = a.reshape(seq // CHUNK, bsz, CHUNK, w)
    return jnp.transpose(a, (1, 0, 2, 3)).reshape(bsz, seq, w)


def kernel(x_prompt, x_sample, p_prompt, p_sample, state_s5_re, state_s5_im, state_gla, cache_conv, w_in, b_in, s5_a_re, s5_a_im, s5_log_dt, s5_b_re, s5_b_im, s5_c_re, s5_c_im, s5_d, w_glu, b_glu, gla_w_a2, gla_b_a, gla_norm_g, conv_w, conv_b, conv_ln_g, conv_ln_b, p_s5, p_gla, p_conv, w_o, w_pg, w_pe, ln_g, ln_b):
    bp, seq, _ = x_prompt.shape
    bs, dseq, _ = x_sample.shape
    depth = w_in.shape[0]
    assert seq % CHUNK == 0 and (bp * CHUNK) == SEQ_ROWS and (bs * dseq) % SEQ_ROWS == 0 and dseq == 8

    fused_w, fused_b = _fused_in_weights(w_in, b_in)
    prep = _s5_prep(s5_a_re, s5_a_im, s5_log_dt, s5_b_re, s5_b_im)

    hp = _to_chunk_major(x_prompt, bp, seq)
    hs = x_sample.reshape(bs * dseq, D_MODEL)
    hpb, hsb = hp.astype(BF16), hs.astype(BF16)
    pr_re, pr_im, pr_gla, pr_conv = [], [], [], []
    sm_re, sm_im, sm_gla, sm_conv = [], [], [], []
    for i in range(depth):
        lw = _layer_weights(i, fused_w, fused_b, prep, s5_c_re, s5_c_im, s5_d, w_glu, b_glu, gla_w_a2, gla_b_a,
                            gla_norm_g, conv_w, conv_b, conv_ln_g, conv_ln_b, p_s5, p_gla, p_conv,
                            w_o, w_pg, w_pe, ln_g, ln_b)
        pp = _to_chunk_major(p_prompt[i], bp, seq).astype(BF16)
        ps = p_sample[i].reshape(bs * dseq, PLE_DIM).astype(BF16)
        hp, hpb, a, b, c, d = _layer(hp, hpb, pp, lw, None, None, None, None, prompt=True)
        pr_re.append(a); pr_im.append(b); pr_gla.append(c); pr_conv.append(d)
        hs, hsb, a, b, c, d = _layer(hs, hsb, ps, lw,
                                     state_s5_re[i].reshape(bs, N_STATE), state_s5_im[i].reshape(bs, N_STATE),
                                     state_gla[i], cache_conv[i], prompt=False)
        sm_re.append(a); sm_im.append(b); sm_gla.append(c); sm_conv.append(d)
    s5_shape = lambda a, n: jnp.stack(a).reshape(depth, n, SSM_GROUPS, SSM_STATE)
    return (_from_chunk_major(hp, bp, seq), hs.reshape(bs, dseq, D_MODEL),
            s5_shape(pr_re, bp), s5_shape(pr_im, bp), jnp.stack(pr_gla), jnp.stack(pr_conv),
            s5_shape(sm_re, bs), s5_shape(sm_im, bs), jnp.stack(sm_gla), jnp.stack(sm_conv))
```

```python
import functools

import numpy as np
import jax
import jax.numpy as jnp
from jax import lax
from jax.experimental import pallas as pl
from jax.experimental.pallas import tpu as pltpu

F32 = jnp.float32
BF16 = jnp.bfloat16

D_MODEL = 2048
PLE_DIM = 256
SSM_WIDTH = 1024
SSM_GROUP = 16
SSM_GROUPS = 64
SSM_STATE = 64
N_STATE = SSM_GROUPS * SSM_STATE
GLA_HEADS = 4
GLA_DK = 128
GLA_DV = 256
GLA_KEY_WIDTH = 512
GLA_VAL_WIDTH = 1024
GLA_GATE_RANK = 16
GLA_TAU = 16.0
CONV_WIDTH = 1024
CONV_K = 31
N_BRANCH = 3
DN_ALPHA = 8.0 ** 0.25
LN_EPS = 1e-5

SUBLANES = 8
LANES = 128
CONV_HIST = 32
CHUNK = 64
SEQ_ROWS = 256
GLA_SAMPLE_ROWS = 64
S5_SLAB = 1024
S5_USLAB = 256
N_SLAB = N_STATE // S5_SLAB
MERGE_ROWS = 256

VMEM_LIMIT = 56 * 1024 * 1024

PROJ_BN = 1024
PROJ_SUB = 256
COL_GATES = 0
COL_S5_U = 6144
COL_V = 7168
COL_CA = 8192
COL_S5_Z = 9216
COL_GLA_Z = 10240
COL_CZ = 11264
COL_CB = 12288
COL_Q = 13312
COL_K = 13824
PROJ_COLS = 14336
ALOW_PAD = LANES

_IN_SIZES = [SSM_WIDTH, SSM_WIDTH, GLA_KEY_WIDTH, GLA_KEY_WIDTH, GLA_VAL_WIDTH, GLA_GATE_RANK, GLA_VAL_WIDTH,
             CONV_WIDTH, CONV_WIDTH, CONV_WIDTH, N_BRANCH * D_MODEL]
_IN_NAMES = ['s5_u', 's5_z', 'q', 'k', 'v', 'alow', 'gla_z', 'ca', 'cb', 'cz', 'gates']
_IN_OFFS = np.concatenate([[0], np.cumsum(_IN_SIZES)])
_IN_SRC = {n: (int(_IN_OFFS[i]), int(_IN_OFFS[i + 1])) for i, n in enumerate(_IN_NAMES)}
_IN_ORDER = ['gates', 's5_u', 'v', 'ca', 's5_z', 'gla_z', 'cz', 'cb', 'q', 'k']
N_IN = int(_IN_OFFS[-1])


def _layer_spec(layer, shape):
    zeros = (0,) * len(shape)
    return pl.BlockSpec((None,) + tuple(shape), lambda *_: (layer,) + zeros, pipeline_mode=pl.Buffered(1))


def _const_spec(shape):
    zeros = (0,) * len(shape)
    return pl.BlockSpec(tuple(shape), lambda *_: zeros, pipeline_mode=pl.Buffered(1))


def _params(sem):
    return pltpu.CompilerParams(dimension_semantics=sem, vmem_limit_bytes=VMEM_LIMIT)


RELAYOUT_ROWS = 128


def _relayout_body(w_ref, o_ref):
    dst = 0
    for name in _IN_ORDER:
        lo, hi = _IN_SRC[name]
        o_ref[:, dst:dst + hi - lo] = w_ref[:, lo:hi].astype(o_ref.dtype)
        dst += hi - lo


def _relayout_in_weights(w_in):
    depth = w_in.shape[0]
    return pl.pallas_call(
        _relayout_body,
        grid=(depth, D_MODEL // RELAYOUT_ROWS),
        in_specs=[pl.BlockSpec((None, RELAYOUT_ROWS, N_IN), lambda l, r: (l, r, 0))],
        out_specs=pl.BlockSpec((None, RELAYOUT_ROWS, PROJ_COLS), lambda l, r: (l, r, 0)),
        out_shape=jax.ShapeDtypeStruct((depth, D_MODEL, PROJ_COLS), BF16),
        compiler_params=_params(("arbitrary", "arbitrary")),
        name="relayout_w_in",
    )(w_in)


def _reorder_bias(b_in):
    parts = [b_in[:, _IN_SRC[n][0]:_IN_SRC[n][1]] for n in _IN_ORDER]
    return jnp.concatenate(parts, axis=-1)[:, None, :]


def _inproj_body(x_ref, w_ref, b_ref, o_ref):
    j = pl.program_id(1)
    is_sig = (j < COL_S5_U // PROJ_BN) | (j == COL_CB // PROJ_BN)
    is_silu = (j >= COL_S5_Z // PROJ_BN) & (j < COL_CB // PROJ_BN)
    for r in range(0, x_ref.shape[0], PROJ_SUB):
        acc = jnp.dot(x_ref[r:r + PROJ_SUB, :], w_ref[...], preferred_element_type=F32) + b_ref[...]
        sig = jax.nn.sigmoid(acc)
        out = jnp.where(is_sig, sig, jnp.where(is_silu, acc * sig, acc))
        o_ref[r:r + PROJ_SUB, :] = out.astype(o_ref.dtype)


def _inproj(layer, x, wts):
    m = x.shape[0]
    bm = min(m, 1024)
    return pl.pallas_call(
        _inproj_body,
        grid=(m // bm, PROJ_COLS // PROJ_BN),
        in_specs=[pl.BlockSpec((bm, D_MODEL), lambda i, j: (i, 0)),
                  pl.BlockSpec((None, D_MODEL, PROJ_BN), lambda i, j: (layer, 0, j)),
                  pl.BlockSpec((None, 1, PROJ_BN), lambda i, j: (layer, 0, j))],
        out_specs=pl.BlockSpec((bm, PROJ_BN), lambda i, j: (i, j)),
        out_shape=jax.ShapeDtypeStruct((m, PROJ_COLS), BF16),
        compiler_params=_params(("arbitrary", "arbitrary")),
        name="inproj",
    )(x, wts['w_in'], wts['b_in'])


def _s5_prep_body(ar_ref, ai_ref, ldt_ref, bre_ref, bim_ref, apr_ref, api_ref, anr_ref, ani_ref, bbr_ref, bbi_ref):
    ar = ar_ref[0]
    ai = ai_ref[0]
    dt = jnp.exp(ldt_ref[0])
    xr = ar * dt
    xi = ai * dt
    t = lax.broadcasted_iota(jnp.int32, (CHUNK, N_STATE), 0).astype(F32)
    ang = t * xi
    cs = jnp.cos(ang)
    sn = jnp.sin(ang)
    mag = jnp.exp(t * xr)
    imag = jnp.exp(-(t * xr))
    apr_ref[0] = mag * cs
    api_ref[0] = mag * sn
    anr_ref[0] = imag * cs
    ani_ref[0] = -(imag * sn)
    e1 = jnp.exp(xr)
    nr = e1 * jnp.cos(xi) - 1.0
    ni = e1 * jnp.sin(xi)
    den = ar * ar + ai * ai
    cr = (nr * ar + ni * ai) / den
    ci = (ni * ar - nr * ai) / den
    bre = bre_ref[0]
    bim = bim_ref[0]
    bbr_ref[0] = cr * bre - ci * bim
    bbi_ref[0] = cr * bim + ci * bre


def _s5_prep(a_re, a_im, log_dt, b_re, b_im):
    depth = a_re.shape[0]
    ar = a_re.reshape(depth, 1, N_STATE)
    ai = a_im.reshape(depth, 1, N_STATE)
    ldt = jnp.broadcast_to(log_dt[:, :, None], (depth, SSM_GROUPS, SSM_STATE)).reshape(depth, 1, N_STATE)
    bre = jnp.transpose(b_re, (0, 3, 1, 2)).reshape(depth, SSM_GROUP, N_STATE)
    bim = jnp.transpose(b_im, (0, 3, 1, 2)).reshape(depth, SSM_GROUP, N_STATE)
    row = pl.BlockSpec((1, 1, N_STATE), lambda i: (i, 0, 0))
    grp = pl.BlockSpec((1, SSM_GROUP, N_STATE), lambda i: (i, 0, 0))
    tab = pl.BlockSpec((1, CHUNK, N_STATE), lambda i: (i, 0, 0))
    tab_shape = jax.ShapeDtypeStruct((depth, CHUNK, N_STATE), F32)
    grp_shape = jax.ShapeDtypeStruct((depth, SSM_GROUP, N_STATE), F32)
    return pl.pallas_call(
        _s5_prep_body,
        grid=(depth,),
        in_specs=[row, row, row, grp, grp],
        out_specs=[tab, tab, tab, tab, grp, grp],
        out_shape=[tab_shape, tab_shape, tab_shape, tab_shape, grp_shape, grp_shape],
        compiler_params=_params(("arbitrary",)),
        name="s5_prep",
    )(ar, ai, ldt, bre, bim)


def _s5_block_weights(bbar, c):
    depth = bbar.shape[0]
    gps = SSM_GROUPS // N_SLAB
    eye = jnp.eye(gps, dtype=F32)
    b4 = bbar.reshape(depth, SSM_GROUP, N_SLAB, gps, SSM_STATE)
    b_blk = jnp.einsum('dinlp,lm->dnlimp', b4, eye)
    b_blk = b_blk.reshape(depth, N_SLAB, S5_USLAB, S5_SLAB)
    c4 = c.reshape(depth, N_SLAB, gps, SSM_GROUP, SSM_STATE)
    c_blk = jnp.einsum('dnlip,lm->dnlpmi', c4, eye)
    c_blk = c_blk.reshape(depth, N_SLAB, S5_SLAB, S5_USLAB)
    return b_blk.astype(BF16), c_blk.astype(BF16)


def _cmul(ar, ai, br, bi):
    return ar * br - ai * bi, ar * bi + ai * br


def _s5_body(*refs, lt, nsub, prompt):
    if prompt:
        (u_ref, sz_ref, bre_ref, bim_ref, cre_ref, cim_ref, tri_ref, apr_ref, api_ref, anr_ref, ani_ref,
         d_ref, wglu_ref, bglu_ref, o_ref, hre_ref, him_ref, y_scr) = refs
        pre_ref, pim_ref = hre_ref, him_ref

        @pl.when(pl.program_id(0) == 0)
        def _():
            hre_ref[...] = jnp.zeros_like(hre_ref)
            him_ref[...] = jnp.zeros_like(him_ref)
    else:
        (u_ref, sz_ref, bre_ref, bim_ref, cre_ref, cim_ref, tri_ref, apr_ref, api_ref, anr_ref, ani_ref,
         d_ref, wglu_ref, bglu_ref, pre_ref, pim_ref, o_ref, hre_ref, him_ref, y_scr) = refs
    rows = nsub * lt
    tri = tri_ref[...]
    for nt in range(N_SLAB):
        sl = slice(nt * S5_SLAB, (nt + 1) * S5_SLAB)
        u_nt = u_ref[:, nt * S5_USLAB:(nt + 1) * S5_USLAB]
        bu_re = jnp.dot(u_nt, bre_ref[nt], preferred_element_type=F32).reshape(nsub, lt, S5_SLAB)
        bu_im = jnp.dot(u_nt, bim_ref[nt], preferred_element_type=F32).reshape(nsub, lt, S5_SLAB)
        w_re, w_im = _cmul(bu_re, bu_im, anr_ref[0:lt, sl][None], ani_ref[0:lt, sl][None])
        w_re = w_re.reshape(rows, S5_SLAB).astype(BF16)
        w_im = w_im.reshape(rows, S5_SLAB).astype(BF16)
        acc_re = jnp.dot(tri, w_re, preferred_element_type=F32).reshape(nsub, lt, S5_SLAB)
        acc_im = jnp.dot(tri, w_im, preferred_element_type=F32).reshape(nsub, lt, S5_SLAB)
        c_re, c_im = _cmul(apr_ref[1:2, sl], api_ref[1:2, sl], pre_ref[:, sl], pim_ref[:, sl])
        acc_re = acc_re + c_re[:, None, :]
        acc_im = acc_im + c_im[:, None, :]
        h_re, h_im = _cmul(acc_re, acc_im, apr_ref[0:lt, sl][None], api_ref[0:lt, sl][None])
        hre_ref[:, sl] = h_re[:, lt - 1, :]
        him_ref[:, sl] = h_im[:, lt - 1, :]
        h_re = h_re.reshape(rows, S5_SLAB).astype(BF16)
        h_im = h_im.reshape(rows, S5_SLAB).astype(BF16)
        y_scr[:, nt * S5_USLAB:(nt + 1) * S5_USLAB] = (
            jnp.dot(h_re, cre_ref[nt], preferred_element_type=F32)
            - jnp.dot(h_im, cim_ref[nt], preferred_element_type=F32))
    y = y_scr[...] + d_ref[...] * u_ref[...].astype(F32)
    y = jax.nn.gelu(y)
    gate = jnp.dot(y.astype(BF16), wglu_ref[...], preferred_element_type=F32) + bglu_ref[...]
    y = y * jax.nn.sigmoid(gate)
    o_ref[...] = (y * sz_ref[...].astype(F32)).astype(o_ref.dtype)


def _block_tri(rows, lt):
    r = np.arange(rows)
    same = (r[:, None] // lt) == (r[None, :] // lt)
    return jnp.asarray(np.where(same & (r[:, None] >= r[None, :]), 1.0, 0.0), dtype=BF16)


def _s5_branch(layer, proj, wts, h0_re, h0_im, *, lt, prompt):
    m = proj.shape[0]
    nsub = SEQ_ROWS // lt
    steps = m // SEQ_ROWS
    seqs = nsub if prompt else m // lt
    tri = _block_tri(SEQ_ROWS, lt)
    col = lambda off, w: pl.BlockSpec((SEQ_ROWS, w), lambda c: (c, off // w))
    tab = _layer_spec(layer, (CHUNK, N_STATE))
    in_specs = [col(COL_S5_U, SSM_WIDTH), col(COL_S5_Z, SSM_WIDTH),
                _layer_spec(layer, (N_SLAB, S5_USLAB, S5_SLAB)), _layer_spec(layer, (N_SLAB, S5_USLAB, S5_SLAB)),
                _layer_spec(layer, (N_SLAB, S5_SLAB, S5_USLAB)), _layer_spec(layer, (N_SLAB, S5_SLAB, S5_USLAB)),
                _const_spec((SEQ_ROWS, SEQ_ROWS)), tab, tab, tab, tab,
                _layer_spec(layer, (1, SSM_WIDTH)), _layer_spec(layer, (SSM_WIDTH, SSM_WIDTH)),
                _layer_spec(layer, (1, SSM_WIDTH))]
    args = [proj, proj, wts['s5_bre'], wts['s5_bim'], wts['s5_cre'], wts['s5_cim'], tri,
            wts['s5_apr'], wts['s5_api'], wts['s5_anr'], wts['s5_ani'], wts['s5_d'], wts['w_glu'], wts['b_glu']]
    if prompt:
        state_spec = pl.BlockSpec((nsub, N_STATE), lambda c: (0, 0))
    else:
        state_spec = pl.BlockSpec((nsub, N_STATE), lambda c: (c, 0))
        state_in = pl.BlockSpec((None, nsub, N_STATE), lambda c: (layer, c, 0))
        in_specs += [state_in, state_in]
        args += [h0_re, h0_im]
    state_shape = jax.ShapeDtypeStruct((seqs, N_STATE), F32)
    return pl.pallas_call(
        functools.partial(_s5_body, lt=lt, nsub=nsub, prompt=prompt),
        grid=(steps,),
        in_specs=in_specs,
        out_specs=[pl.BlockSpec((SEQ_ROWS, SSM_WIDTH), lambda c: (c, 0)), state_spec, state_spec],
        out_shape=[jax.ShapeDtypeStruct((m, SSM_WIDTH), BF16), state_shape, state_shape],
        scratch_shapes=[pltpu.VMEM((SEQ_ROWS, SSM_WIDTH), F32)],
        compiler_params=_params(("arbitrary",)),
        name="s5_prompt" if prompt else "s5_sample",
    )(*args)


def _gla_body(*refs, lt, nsub, prompt):
    if prompt:
        (x_ref, q_ref, k_ref, v_ref, gz_ref, wal_ref, bal_ref, wa2_ref, ba_ref, ng_ref, tri_ref,
         o_ref, s_ref) = refs
        s0_ref = s_ref

        @pl.when(pl.program_id(0) == 0)
        def _():
            s_ref[...] = jnp.zeros_like(s_ref)
    else:
        (x_ref, q_ref, k_ref, v_ref, gz_ref, wal_ref, bal_ref, wa2_ref, ba_ref, ng_ref, tri_ref,
         s0_ref, _, o_ref, s_ref) = refs
    tri = tri_ref[...]
    a_low = jnp.dot(x_ref[...], wal_ref[...], preferred_element_type=F32) + bal_ref[...]
    logit = jnp.dot(a_low.astype(BF16), wa2_ref[...], preferred_element_type=F32) + ba_ref[...]
    log_a = jax.nn.log_sigmoid(logit) * (1.0 / GLA_TAU)
    hi = log_a.astype(BF16)
    lo = (log_a - hi.astype(F32)).astype(BF16)
    cum = (jnp.dot(tri, hi, preferred_element_type=F32) + jnp.dot(tri, lo, preferred_element_type=F32))
    q = q_ref[...].astype(F32) * (GLA_DK ** -0.5)
    k = k_ref[...].astype(F32)
    qd = (q * jnp.exp(cum)).astype(BF16)
    kd = (k * jnp.exp(-cum)).astype(BF16)
    cum3 = cum.reshape(nsub, lt, GLA_KEY_WIDTH)
    last = cum3[:, lt - 1:lt, :]
    kk = (k.reshape(nsub, lt, GLA_KEY_WIDTH) * jnp.exp(last - cum3)).astype(BF16)
    e_last = jnp.exp(last)
    ri = lax.broadcasted_iota(jnp.int32, (lt, lt), 0)
    ci = lax.broadcasted_iota(jnp.int32, (lt, lt), 1)
    causal = ri >= ci
    for s in range(nsub):
        rs = slice(s * lt, (s + 1) * lt)
        for h in range(GLA_HEADS):
            ks = slice(h * GLA_DK, (h + 1) * GLA_DK)
            vs = slice(h * GLA_DV, (h + 1) * GLA_DV)
            qd_sh = qd[rs, ks]
            v_sh = v_ref[rs, vs]
            att = lax.dot_general(qd_sh, kd[rs, ks], (((1,), (1,)), ((), ())), preferred_element_type=F32)
            att = jnp.where(causal, att, 0.0).astype(BF16)
            state = s0_ref[s, h]
            o = (jnp.dot(att, v_sh, preferred_element_type=F32)
                 + jnp.dot(qd_sh, state.astype(BF16), preferred_element_type=F32))
            kv = lax.dot_general(kk[s, :, ks], v_sh, (((0,), (0,)), ((), ())), preferred_element_type=F32)
            dec = jnp.transpose(jnp.broadcast_to(e_last[s, :, ks], (GLA_DK, GLA_DK)))
            s_ref[s, h] = jnp.concatenate([dec, dec], axis=1) * state + kv
            mu = jnp.mean(o, axis=-1, keepdims=True)
            var = jnp.mean(jnp.square(o - mu), axis=-1, keepdims=True)
            o = (o - mu) * lax.rsqrt(var + LN_EPS) * ng_ref[:, vs]
            o_ref[rs, vs] = (o * gz_ref[rs, vs].astype(F32)).astype(o_ref.dtype)


def _gla_branch(layer, xb, proj, wts, s0, s_stack, *, lt, prompt):
    m = proj.shape[0]
    rows = SEQ_ROWS if prompt else GLA_SAMPLE_ROWS
    nsub = rows // lt
    steps = m // rows
    tri = _block_tri(rows, lt)
    col = lambda off, w: pl.BlockSpec((rows, w), lambda c: (c, off // w))
    in_specs = [pl.BlockSpec((rows, D_MODEL), lambda c: (c, 0)),
                col(COL_Q, GLA_KEY_WIDTH), col(COL_K, GLA_KEY_WIDTH), col(COL_V, GLA_VAL_WIDTH),
                col(COL_GLA_Z, GLA_VAL_WIDTH),
                _layer_spec(layer, (D_MODEL, ALOW_PAD)), _layer_spec(layer, (1, ALOW_PAD)),
                _layer_spec(layer, (ALOW_PAD, GLA_KEY_WIDTH)), _layer_spec(layer, (1, GLA_KEY_WIDTH)),
                _layer_spec(layer, (1, GLA_VAL_WIDTH)), _const_spec((rows, rows))]
    args = [xb, proj, proj, proj, proj, wts['w_alow'], wts['b_alow'], wts['gla_w_a2'], wts['gla_b_a'],
            wts['gla_norm_g'], tri]
    state_dims = (GLA_HEADS, GLA_DK, GLA_DV)
    aliases = {}
    if prompt:
        state_spec = pl.BlockSpec((nsub,) + state_dims, lambda c: (0, 0, 0, 0))
        state_shape = jax.ShapeDtypeStruct((nsub,) + state_dims, F32)
    else:
        state_spec = pl.BlockSpec((None, nsub) + state_dims, lambda c: (layer, c, 0, 0, 0))
        state_shape = jax.ShapeDtypeStruct(s0.shape, F32)
        in_specs += [state_spec]
        args += [s0]
        if s_stack is not None:
            in_specs += [pl.BlockSpec(memory_space=pl.ANY)]
            args += [s_stack]
            aliases = {len(args) - 1: 1}
    body = functools.partial(_gla_body, lt=lt, nsub=nsub, prompt=prompt)
    if not prompt and s_stack is None:
        inner = body
        body = lambda *refs: inner(*refs[:12], None, *refs[12:])
    return pl.pallas_call(
        body,
        grid=(steps,),
        in_specs=in_specs,
        out_specs=[pl.BlockSpec((rows, GLA_VAL_WIDTH), lambda c: (c, 0)), state_spec],
        out_shape=[jax.ShapeDtypeStruct((m, GLA_VAL_WIDTH), BF16), state_shape],
        input_output_aliases=aliases,
        compiler_params=_params(("arbitrary",)),
        name="gla_prompt" if prompt else "gla_sample",
    )(*args)


def _conv_body(*refs, lt, nsub, prompt):
    if prompt:
        (ca_ref, scb_ref, cz_ref, w_ref, b_ref, lng_ref, lnb_ref, o_ref, cache_out_ref, x_scr, y_scr) = refs

        @pl.when(pl.program_id(0) == 0)
        def _():
            x_scr[:, 0:CONV_HIST, :] = jnp.zeros((nsub, CONV_HIST, CONV_WIDTH), F32)
    else:
        (ca_ref, scb_ref, cz_ref, w_ref, b_ref, lng_ref, lnb_ref, cache_ref, _,
         o_ref, cache_out_ref, x_scr, y_scr) = refs
    pad = CONV_HIST - (CONV_K - 1)
    cw = 2 * LANES if lt > SUBLANES else CONV_WIDTH
    taps_by_shift = [[j for j in range(CONV_K) if (j + pad) % SUBLANES == sh] for sh in range(SUBLANES)]

    def per_seq(s, carry):
        r0 = pl.multiple_of(s * lt, SUBLANES)
        g = ca_ref[pl.ds(r0, lt), :].astype(F32) * scb_ref[pl.ds(r0, lt), :].astype(F32)
        x_scr[s, CONV_HIST:CONV_HIST + lt, :] = g
        if not prompt:
            x_scr[s, pad:CONV_HIST, :] = cache_ref[s]
        for c0 in range(0, CONV_WIDTH, cw):
            acc = None
            for sh, taps in enumerate(taps_by_shift):
                span = max(taps) + pad - sh + lt
                xs = x_scr[s, pl.ds(sh, span), c0:c0 + cw]
                part = None
                for j in taps:
                    base = j + pad - sh
                    term = w_ref[j:j + 1, c0:c0 + cw] * xs[base:base + lt, :]
                    part = term if part is None else part + term
                acc = part if acc is None else acc + part
            y_scr[pl.ds(r0, lt), c0:c0 + cw] = acc
        cache_out_ref[s] = x_scr[s, lt + pad:lt + CONV_HIST, :]
        if prompt:
            x_scr[s, 0:CONV_HIST, :] = x_scr[s, lt:lt + CONV_HIST, :]
        return carry

    lax.fori_loop(0, nsub, per_seq, 0)
    y = y_scr[...] + b_ref[...]
    mu = jnp.mean(y, axis=-1, keepdims=True)
    var = jnp.mean(jnp.square(y - mu), axis=-1, keepdims=True)
    y = (y - mu) * lax.rsqrt(var + LN_EPS) * lng_ref[...] + lnb_ref[...]
    o_ref[...] = (jax.nn.silu(y) * cz_ref[...].astype(F32)).astype(o_ref.dtype)


def _conv_branch(layer, proj, wts, cache, cache_stack, *, lt, prompt):
    m = proj.shape[0]
    nsub = SEQ_ROWS // lt
    steps = m // SEQ_ROWS
    col = lambda off: pl.BlockSpec((SEQ_ROWS, CONV_WIDTH), lambda c: (c, off // CONV_WIDTH))
    in_specs = [col(COL_CA), col(COL_CB), col(COL_CZ),
                _layer_spec(layer, (CONV_K, CONV_WIDTH)), _layer_spec(layer, (1, CONV_WIDTH)),
                _layer_spec(layer, (1, CONV_WIDTH)), _layer_spec(layer, (1, CONV_WIDTH))]
    args = [proj, proj, proj, wts['conv_w'], wts['conv_b'], wts['conv_ln_g'], wts['conv_ln_b']]
    aliases = {}
    if prompt:
        cache_spec = pl.BlockSpec((nsub, CONV_K - 1, CONV_WIDTH), lambda c: (0, 0, 0))
        cache_shape = jax.ShapeDtypeStruct((nsub, CONV_K - 1, CONV_WIDTH), F32)
    else:
        cache_spec = pl.BlockSpec((None, nsub, CONV_K - 1, CONV_WIDTH), lambda c: (layer, c, 0, 0))
        cache_shape = jax.ShapeDtypeStruct(cache.shape, F32)
        in_specs += [cache_spec]
        args += [cache]
        if cache_stack is not None:
            in_specs += [pl.BlockSpec(memory_space=pl.ANY)]
            args += [cache_stack]
            aliases = {len(args) - 1: 1}
    body = functools.partial(_conv_body, lt=lt, nsub=nsub, prompt=prompt)
    if not prompt and cache_stack is None:
        inner = body
        body = lambda *refs: inner(*refs[:8], None, *refs[8:])
    return pl.pallas_call(
        body,
        grid=(steps,),
        in_specs=in_specs,
        out_specs=[pl.BlockSpec((SEQ_ROWS, CONV_WIDTH), lambda c: (c, 0)), cache_spec],
        out_shape=[jax.ShapeDtypeStruct((m, CONV_WIDTH), BF16), cache_shape],
        scratch_shapes=[pltpu.VMEM((nsub, CONV_HIST + lt, CONV_WIDTH), F32),
                        pltpu.VMEM((SEQ_ROWS, CONV_WIDTH), F32)],
        input_output_aliases=aliases,
        compiler_params=_params(("arbitrary",)),
        name="conv_prompt" if prompt else "conv_sample",
    )(*args)


def _merge_body(a1_ref, a2_ref, a3_ref, g1_ref, g2_ref, g3_ref, p1_ref, p2_ref, p3_ref, o_ref):
    m = g1_ref[...].astype(F32) * jnp.dot(a1_ref[...], p1_ref[...], preferred_element_type=F32)
    m = m + g2_ref[...].astype(F32) * jnp.dot(a2_ref[...], p2_ref[...], preferred_element_type=F32)
    m = m + g3_ref[...].astype(F32) * jnp.dot(a3_ref[...], p3_ref[...], preferred_element_type=F32)
    o_ref[...] = m.astype(o_ref.dtype)


def _merge(layer, a_s5, a_gla, a_conv, proj, wts):
    m = a_s5.shape[0]
    act = pl.BlockSpec((MERGE_ROWS, SSM_WIDTH), lambda i: (i, 0))
    gate = lambda n: pl.BlockSpec((MERGE_ROWS, D_MODEL), lambda i: (i, COL_GATES // D_MODEL + n))
    wspec = _layer_spec(layer, (SSM_WIDTH, D_MODEL))
    return pl.pallas_call(
        _merge_body,
        grid=(m // MERGE_ROWS,),
        in_specs=[act, act, act, gate(0), gate(1), gate(2), wspec, wspec, wspec],
        out_specs=pl.BlockSpec((MERGE_ROWS, D_MODEL), lambda i: (i, 0)),
        out_shape=jax.ShapeDtypeStruct((m, D_MODEL), BF16),
        compiler_params=_params(("arbitrary",)),
        name="merge",
    )(a_s5, a_gla, a_conv, proj, proj, proj, wts['p_s5'], wts['p_gla'], wts['p_conv'])


def _post_body(m_ref, x_ref, p_ref, wo_ref, wpg_ref, wpe_ref, g_ref, b_ref, xo_ref, xb_ref):
    h = DN_ALPHA * x_ref[...] + jnp.dot(m_ref[...], wo_ref[...], preferred_element_type=F32)
    gate = jax.nn.sigmoid(jnp.dot(h.astype(BF16), wpg_ref[...], preferred_element_type=F32))
    p = p_ref[...].reshape(MERGE_ROWS, PLE_DIM).astype(BF16)
    h = h + gate * jnp.dot(p, wpe_ref[...], preferred_element_type=F32)
    mu = jnp.mean(h, axis=-1, keepdims=True)
    var = jnp.mean(jnp.square(h - mu), axis=-1, keepdims=True)
    xn = (h - mu) * lax.rsqrt(var + LN_EPS) * g_ref[...] + b_ref[...]
    xo_ref[...] = xn
    xb_ref[...] = xn.astype(BF16)


def _post(layer, merged, x, p, wts, *, prompt):
    m = x.shape[0]
    row = lambda w: pl.BlockSpec((MERGE_ROWS, w), lambda i: (i, 0))
    if prompt:
        p_spec = pl.BlockSpec((None, p.shape[1], CHUNK, PLE_DIM), lambda i: (layer, 0, i, 0))
    else:
        p_spec = pl.BlockSpec((None, MERGE_ROWS, PLE_DIM), lambda i: (layer, i, 0))
    return pl.pallas_call(
        _post_body,
        grid=(m // MERGE_ROWS,),
        in_specs=[row(D_MODEL), row(D_MODEL), p_spec,
                  _layer_spec(layer, (D_MODEL, D_MODEL)), _layer_spec(layer, (D_MODEL, D_MODEL)),
                  _layer_spec(layer, (PLE_DIM, D_MODEL)),
                  _layer_spec(layer, (1, D_MODEL)), _layer_spec(layer, (1, D_MODEL))],
        out_specs=[row(D_MODEL), row(D_MODEL)],
        out_shape=[jax.ShapeDtypeStruct((m, D_MODEL), F32), jax.ShapeDtypeStruct((m, D_MODEL), BF16)],
        compiler_params=_params(("arbitrary",)),
        name="post",
    )(merged, x, p, wts['w_o'], wts['w_pg'], wts['w_pe'], wts['ln_g'], wts['ln_b'])


def _layer(layer, x, xb, p, wts, states, stacks, *, lt, prompt):
    s5_re, s5_im, gla_s, conv_buf = states
    gla_stack, conv_stack = stacks
    proj = _inproj(layer, xb, wts)
    a_s5, h_re, h_im = _s5_branch(layer, proj, wts, s5_re, s5_im, lt=lt, prompt=prompt)
    a_gla, gla_new = _gla_branch(layer, xb, proj, wts, gla_s, gla_stack, lt=lt, prompt=prompt)
    a_conv, conv_new = _conv_branch(layer, proj, wts, conv_buf, conv_stack, lt=lt, prompt=prompt)
    merged = _merge(layer, a_s5, a_gla, a_conv, proj, wts)
    x_new, xb_new = _post(layer, merged, x, p, wts, prompt=prompt)
    return x_new, xb_new, h_re, h_im, gla_new, conv_new


def _to_chunk_major(a, bsz, seq):
    w = a.shape[-1]
    a = a.reshape(bsz, seq // CHUNK, CHUNK, w)
    return jnp.transpose(a, (1, 0, 2, 3)).reshape(bsz * seq, w)


def _from_chunk_major(a, bsz, seq):
    w = a.shape[-1]
    a = a.reshape(seq // CHUNK, bsz, CHUNK, w)
    return jnp.transpose(a, (1, 0, 2, 3)).reshape(bsz, seq, w)


def kernel(x_prompt, x_sample, p_prompt, p_sample, state_s5_re, state_s5_im, state_gla, cache_conv, w_in, b_in, s5_a_re, s5_a_im, s5_log_dt, s5_b_re, s5_b_im, s5_c_re, s5_c_im, s5_d, w_glu, b_glu, gla_w_a2, gla_b_a, gla_norm_g, conv_w, conv_b, conv_ln_g, conv_ln_b, p_s5, p_gla, p_conv, w_o, w_pg, w_pe, ln_g, ln_b):
    bp, seq, _ = x_prompt.shape
    bs, dseq, _ = x_sample.shape
    depth = w_in.shape[0]
    assert seq % CHUNK == 0 and bp * CHUNK == SEQ_ROWS and (bs * dseq) % SEQ_ROWS == 0 and dseq == SUBLANES

    apr, api, anr, ani, bbr, bbi = _s5_prep(s5_a_re, s5_a_im, s5_log_dt, s5_b_re, s5_b_im)
    s5_bre, s5_cre = _s5_block_weights(bbr, s5_c_re)
    s5_bim, s5_cim = _s5_block_weights(bbi, s5_c_im)
    row = lambda a: a[:, None, :]
    alow = slice(*_IN_SRC['alow'])
    pad_cols = lambda a: jnp.pad(a, [(0, 0)] * (a.ndim - 1) + [(0, ALOW_PAD - GLA_GATE_RANK)])
    wts = dict(
        w_in=_relayout_in_weights(w_in), b_in=_reorder_bias(b_in),
        w_alow=pad_cols(w_in[:, :, alow]).astype(BF16), b_alow=row(pad_cols(b_in[:, alow])),
        s5_bre=s5_bre, s5_bim=s5_bim, s5_cre=s5_cre, s5_cim=s5_cim,
        s5_apr=apr, s5_api=api, s5_anr=anr, s5_ani=ani,
        s5_d=row(s5_d), w_glu=w_glu.astype(BF16), b_glu=row(b_glu),
        gla_w_a2=jnp.pad(gla_w_a2, ((0, 0), (0, ALOW_PAD - GLA_GATE_RANK), (0, 0))).astype(BF16),
        gla_b_a=row(gla_b_a), gla_norm_g=row(gla_norm_g),
        conv_w=conv_w, conv_b=row(conv_b), conv_ln_g=row(conv_ln_g), conv_ln_b=row(conv_ln_b),
        p_s5=p_s5.astype(BF16), p_gla=p_gla.astype(BF16), p_conv=p_conv.astype(BF16),
        w_o=w_o.astype(BF16), w_pg=w_pg.astype(BF16), w_pe=w_pe.astype(BF16),
        ln_g=row(ln_g), ln_b=row(ln_b))

    hp = _to_chunk_major(x_prompt, bp, seq)
    hs = x_sample.reshape(bs * dseq, D_MODEL)
    hpb, hsb = hp.astype(BF16), hs.astype(BF16)
    ps = p_sample.reshape(depth, bs * dseq, PLE_DIM)
    sample_states = (state_s5_re.reshape(depth, bs, N_STATE), state_s5_im.reshape(depth, bs, N_STATE),
                     state_gla, cache_conv)
    pr_re, pr_im, pr_gla, pr_conv, sm_re, sm_im = [], [], [], [], [], []
    gla_stack = conv_stack = None
    for i in range(depth):
        hp, hpb, a, b, c, d = _layer(i, hp, hpb, p_prompt, wts, (None,) * 4, (None, None), lt=CHUNK, prompt=True)
        pr_re.append(a); pr_im.append(b); pr_gla.append(c); pr_conv.append(d)
        hs, hsb, a, b, gla_stack, conv_stack = _layer(i, hs, hsb, ps, wts, sample_states, (gla_stack, conv_stack),
                                                      lt=dseq, prompt=False)
        sm_re.append(a); sm_im.append(b)
    s5_shape = lambda a, n: jnp.stack(a).reshape(depth, n, SSM_GROUPS, SSM_STATE)
    return (_from_chunk_major(hp, bp, seq), hs.reshape(bs, dseq, D_MODEL),
            s5_shape(pr_re, bp), s5_shape(pr_im, bp), jnp.stack(pr_gla), jnp.stack(pr_conv),
            s5_shape(sm_re, bs), s5_shape(sm_im, bs), gla_stack, conv_stack)
```

```python
import functools

import numpy as np
import jax
import jax.numpy as jnp
from jax import lax
from jax.experimental import pallas as pl
from jax.experimental.pallas import tpu as pltpu

F32 = jnp.float32
BF16 = jnp.bfloat16

D_MODEL = 2048
PLE_DIM = 256
SSM_WIDTH = 1024
SSM_GROUP = 16
SSM_GROUPS = 64
SSM_STATE = 64
N_STATE = SSM_GROUPS * SSM_STATE
GLA_HEADS = 4
GLA_DK = 128
GLA_DV = 256
GLA_KEY_WIDTH = 512
GLA_VAL_WIDTH = 1024
GLA_GATE_RANK = 16
GLA_TAU = 16.0
CONV_WIDTH = 1024
CONV_K = 31
N_BRANCH = 3
DN_ALPHA = 8.0 ** 0.25
LN_EPS = 1e-5

SUBLANES = 8
LANES = 128
CONV_HIST = 32
CHUNK = 64
SEQ_ROWS = 256
GLA_SAMPLE_ROWS = 64
S5_SLAB = 1024
S5_USLAB = 256
N_SLAB = N_STATE // S5_SLAB
MERGE_ROWS = 512

VMEM_LIMIT = 56 * 1024 * 1024

PROJ_BN = 1024
PROJ_BM = 2048
PROJ_SUB = 512
COL_GATES = 0
COL_S5_U = 6144
COL_V = 7168
COL_CA = 8192
COL_S5_Z = 9216
COL_GLA_Z = 10240
COL_CZ = 11264
COL_CB = 12288
COL_Q = 13312
COL_K = 13824
PROJ_COLS = 14336
ALOW_PAD = LANES

_IN_SIZES = [SSM_WIDTH, SSM_WIDTH, GLA_KEY_WIDTH, GLA_KEY_WIDTH, GLA_VAL_WIDTH, GLA_GATE_RANK, GLA_VAL_WIDTH,
             CONV_WIDTH, CONV_WIDTH, CONV_WIDTH, N_BRANCH * D_MODEL]
_IN_NAMES = ['s5_u', 's5_z', 'q', 'k', 'v', 'alow', 'gla_z', 'ca', 'cb', 'cz', 'gates']
_IN_OFFS = np.concatenate([[0], np.cumsum(_IN_SIZES)])
_IN_SRC = {n: (int(_IN_OFFS[i]), int(_IN_OFFS[i + 1])) for i, n in enumerate(_IN_NAMES)}
_IN_ORDER = ['gates', 's5_u', 'v', 'ca', 's5_z', 'gla_z', 'cz', 'cb', 'q', 'k']
N_IN = int(_IN_OFFS[-1])


def _layer_spec(layer, shape):
    zeros = (0,) * len(shape)
    return pl.BlockSpec((None,) + tuple(shape), lambda *_: (layer,) + zeros, pipeline_mode=pl.Buffered(1))


def _const_spec(shape):
    zeros = (0,) * len(shape)
    return pl.BlockSpec(tuple(shape), lambda *_: zeros, pipeline_mode=pl.Buffered(1))


def _params(sem):
    return pltpu.CompilerParams(dimension_semantics=sem, vmem_limit_bytes=VMEM_LIMIT)


RELAYOUT_ROWS = 512


def _relayout_body(tbl_ref, w_ref, o_ref):
    o_ref[...] = w_ref[0].astype(o_ref.dtype)


def _relayout_in_weights(w_in_t):
    depth = w_in_t.shape[0]
    src = []
    for name in _IN_ORDER:
        lo, hi = _IN_SRC[name]
        src += list(range(lo, hi, RELAYOUT_ROWS))
    assert all(lo % GLA_GATE_RANK == 0 for lo in src)
    table = jnp.asarray(np.asarray(src, np.int32) // GLA_GATE_RANK)
    grid_spec = pltpu.PrefetchScalarGridSpec(
        num_scalar_prefetch=1,
        grid=(depth, PROJ_COLS // RELAYOUT_ROWS),
        in_specs=[pl.BlockSpec((pl.Element(1), pl.Element(RELAYOUT_ROWS), pl.Element(D_MODEL)),
                               lambda l, j, tbl: (l, tbl[j] * GLA_GATE_RANK, 0))],
        out_specs=pl.BlockSpec((None, RELAYOUT_ROWS, D_MODEL), lambda l, j, tbl: (l, j, 0)))
    return pl.pallas_call(
        _relayout_body,
        grid_spec=grid_spec,
        out_shape=jax.ShapeDtypeStruct((depth, PROJ_COLS, D_MODEL), BF16),
        compiler_params=_params(("arbitrary", "arbitrary")),
        name="relayout_w_in",
    )(table, w_in_t)


def _reorder_bias(b_in):
    parts = [b_in[:, _IN_SRC[n][0]:_IN_SRC[n][1]] for n in _IN_ORDER]
    return jnp.concatenate(parts, axis=-1)[:, None, :]


_NT = (((1,), (1,)), ((), ()))


def _inproj_body(x_ref, w_ref, b_ref, o_ref):
    j = pl.program_id(1)
    is_sig = (j < COL_S5_U // PROJ_BN) | (j == COL_CB // PROJ_BN)
    is_silu = (j >= COL_S5_Z // PROJ_BN) & (j < COL_CB // PROJ_BN)
    for r in range(0, x_ref.shape[0], PROJ_SUB):
        acc = lax.dot_general(x_ref[r:r + PROJ_SUB, :], w_ref[...], _NT, preferred_element_type=F32) + b_ref[...]
        sig = jax.nn.sigmoid(acc)
        out = jnp.where(is_sig, sig, jnp.where(is_silu, acc * sig, acc))
        o_ref[r:r + PROJ_SUB, :] = out.astype(o_ref.dtype)


def _inproj(layer, x, wts):
    m = x.shape[0]
    bm = min(m, PROJ_BM)
    return pl.pallas_call(
        _inproj_body,
        grid=(m // bm, PROJ_COLS // PROJ_BN),
        in_specs=[pl.BlockSpec((bm, D_MODEL), lambda i, j: (i, 0)),
                  pl.BlockSpec((None, PROJ_BN, D_MODEL), lambda i, j: (layer, j, 0)),
                  pl.BlockSpec((None, 1, PROJ_BN), lambda i, j: (layer, 0, j))],
        out_specs=pl.BlockSpec((bm, PROJ_BN), lambda i, j: (i, j)),
        out_shape=jax.ShapeDtypeStruct((m, PROJ_COLS), BF16),
        compiler_params=_params(("arbitrary", "arbitrary")),
        name="inproj",
    )(x, wts['w_in'], wts['b_in'])


def _s5_prep_body(ar_ref, ai_ref, ldt_ref, bre_ref, bim_ref, apr_ref, api_ref, anr_ref, ani_ref, bbr_ref, bbi_ref):
    ar = ar_ref[0]
    ai = ai_ref[0]
    dt = jnp.exp(ldt_ref[0])
    xr = ar * dt
    xi = ai * dt
    t = lax.broadcasted_iota(jnp.int32, (CHUNK, N_STATE), 0).astype(F32)
    ang = t * xi
    cs = jnp.cos(ang)
    sn = jnp.sin(ang)
    mag = jnp.exp(t * xr)
    imag = jnp.exp(-(t * xr))
    apr_ref[0] = mag * cs
    api_ref[0] = mag * sn
    anr_ref[0] = imag * cs
    ani_ref[0] = -(imag * sn)
    e1 = jnp.exp(xr)
    nr = e1 * jnp.cos(xi) - 1.0
    ni = e1 * jnp.sin(xi)
    den = ar * ar + ai * ai
    cr = (nr * ar + ni * ai) / den
    ci = (ni * ar - nr * ai) / den
    bre = bre_ref[0]
    bim = bim_ref[0]
    bbr_ref[0] = cr * bre - ci * bim
    bbi_ref[0] = cr * bim + ci * bre


def _s5_prep(a_re, a_im, log_dt, b_re, b_im):
    depth = a_re.shape[0]
    ar = a_re.reshape(depth, 1, N_STATE)
    ai = a_im.reshape(depth, 1, N_STATE)
    ldt = jnp.broadcast_to(log_dt[:, :, None], (depth, SSM_GROUPS, SSM_STATE)).reshape(depth, 1, N_STATE)
    bre = jnp.transpose(b_re, (0, 3, 1, 2)).reshape(depth, SSM_GROUP, N_STATE)
    bim = jnp.transpose(b_im, (0, 3, 1, 2)).reshape(depth, SSM_GROUP, N_STATE)
    row = pl.BlockSpec((1, 1, N_STATE), lambda i: (i, 0, 0))
    grp = pl.BlockSpec((1, SSM_GROUP, N_STATE), lambda i: (i, 0, 0))
    tab = pl.BlockSpec((1, CHUNK, N_STATE), lambda i: (i, 0, 0))
    tab_shape = jax.ShapeDtypeStruct((depth, CHUNK, N_STATE), F32)
    grp_shape = jax.ShapeDtypeStruct((depth, SSM_GROUP, N_STATE), F32)
    return pl.pallas_call(
        _s5_prep_body,
        grid=(depth,),
        in_specs=[row, row, row, grp, grp],
        out_specs=[tab, tab, tab, tab, grp, grp],
        out_shape=[tab_shape, tab_shape, tab_shape, tab_shape, grp_shape, grp_shape],
        compiler_params=_params(("arbitrary",)),
        name="s5_prep",
    )(ar, ai, ldt, bre, bim)


def _s5_block_weights(bbar, c):
    depth = bbar.shape[0]
    gps = SSM_GROUPS // N_SLAB
    eye = jnp.eye(gps, dtype=F32)
    b4 = bbar.reshape(depth, SSM_GROUP, N_SLAB, gps, SSM_STATE)
    b_blk = jnp.einsum('dinlp,lm->dnlimp', b4, eye)
    b_blk = b_blk.reshape(depth, N_SLAB, S5_USLAB, S5_SLAB)
    c4 = c.reshape(depth, N_SLAB, gps, SSM_GROUP, SSM_STATE)
    c_blk = jnp.einsum('dnlip,lm->dnlpmi', c4, eye)
    c_blk = c_blk.reshape(depth, N_SLAB, S5_SLAB, S5_USLAB)
    return b_blk.astype(BF16), c_blk.astype(BF16)


def _cmul(ar, ai, br, bi):
    return ar * br - ai * bi, ar * bi + ai * br


def _s5_body(*refs, lt, nsub, prompt):
    if prompt:
        (u_ref, sz_ref, bre_ref, bim_ref, cre_ref, cim_ref, tri_ref, apr_ref, api_ref, anr_ref, ani_ref,
         d_ref, wglu_ref, bglu_ref, o_ref, hre_ref, him_ref, y_scr) = refs
        pre_ref, pim_ref = hre_ref, him_ref

        @pl.when(pl.program_id(0) == 0)
        def _():
            hre_ref[...] = jnp.zeros_like(hre_ref)
            him_ref[...] = jnp.zeros_like(him_ref)
    else:
        (u_ref, sz_ref, bre_ref, bim_ref, cre_ref, cim_ref, tri_ref, apr_ref, api_ref, anr_ref, ani_ref,
         d_ref, wglu_ref, bglu_ref, pre_ref, pim_ref, o_ref, hre_ref, him_ref, y_scr) = refs
    rows = nsub * lt
    tri = tri_ref[...]
    for nt in range(N_SLAB):
        sl = slice(nt * S5_SLAB, (nt + 1) * S5_SLAB)
        u_nt = u_ref[:, nt * S5_USLAB:(nt + 1) * S5_USLAB]
        bu_re = jnp.dot(u_nt, bre_ref[nt], preferred_element_type=F32).reshape(nsub, lt, S5_SLAB)
        bu_im = jnp.dot(u_nt, bim_ref[nt], preferred_element_type=F32).reshape(nsub, lt, S5_SLAB)
        w_re, w_im = _cmul(bu_re, bu_im, anr_ref[0:lt, sl][None], ani_ref[0:lt, sl][None])
        w_re = w_re.reshape(rows, S5_SLAB).astype(BF16)
        w_im = w_im.reshape(rows, S5_SLAB).astype(BF16)
        acc_re = jnp.dot(tri, w_re, preferred_element_type=F32).reshape(nsub, lt, S5_SLAB)
        acc_im = jnp.dot(tri, w_im, preferred_element_type=F32).reshape(nsub, lt, S5_SLAB)
        c_re, c_im = _cmul(apr_ref[1:2, sl], api_ref[1:2, sl], pre_ref[:, sl], pim_ref[:, sl])
        acc_re = acc_re + c_re[:, None, :]
        acc_im = acc_im + c_im[:, None, :]
        h_re, h_im = _cmul(acc_re, acc_im, apr_ref[0:lt, sl][None], api_ref[0:lt, sl][None])
        hre_ref[:, sl] = h_re[:, lt - 1, :]
        him_ref[:, sl] = h_im[:, lt - 1, :]
        h_re = h_re.reshape(rows, S5_SLAB).astype(BF16)
        h_im = h_im.reshape(rows, S5_SLAB).astype(BF16)
        y_scr[:, nt * S5_USLAB:(nt + 1) * S5_USLAB] = (
            jnp.dot(h_re, cre_ref[nt], preferred_element_type=F32)
            - jnp.dot(h_im, cim_ref[nt], preferred_element_type=F32))
    y = y_scr[...] + d_ref[...] * u_ref[...].astype(F32)
    y = jax.nn.gelu(y)
    gate = jnp.dot(y.astype(BF16), wglu_ref[...], preferred_element_type=F32) + bglu_ref[...]
    y = y * jax.nn.sigmoid(gate)
    o_ref[...] = (y * sz_ref[...].astype(F32)).astype(o_ref.dtype)


def _block_tri(rows, lt):
    r = np.arange(rows)
    same = (r[:, None] // lt) == (r[None, :] // lt)
    return jnp.asarray(np.where(same & (r[:, None] >= r[None, :]), 1.0, 0.0), dtype=BF16)


def _s5_branch(layer, proj, wts, h0_re, h0_im, *, lt, prompt):
    m = proj.shape[0]
    nsub = SEQ_ROWS // lt
    steps = m // SEQ_ROWS
    seqs = nsub if prompt else m // lt
    tri = _block_tri(SEQ_ROWS, lt)
    col = lambda off, w: pl.BlockSpec((SEQ_ROWS, w), lambda c: (c, off // w))
    tab = _layer_spec(layer, (CHUNK, N_STATE))
    in_specs = [col(COL_S5_U, SSM_WIDTH), col(COL_S5_Z, SSM_WIDTH),
                _layer_spec(layer, (N_SLAB, S5_USLAB, S5_SLAB)), _layer_spec(layer, (N_SLAB, S5_USLAB, S5_SLAB)),
                _layer_spec(layer, (N_SLAB, S5_SLAB, S5_USLAB)), _layer_spec(layer, (N_SLAB, S5_SLAB, S5_USLAB)),
                _const_spec((SEQ_ROWS, SEQ_ROWS)), tab, tab, tab, tab,
                _layer_spec(layer, (1, SSM_WIDTH)), _layer_spec(layer, (SSM_WIDTH, SSM_WIDTH)),
                _layer_spec(layer, (1, SSM_WIDTH))]
    args = [proj, proj, wts['s5_bre'], wts['s5_bim'], wts['s5_cre'], wts['s5_cim'], tri,
            wts['s5_apr'], wts['s5_api'], wts['s5_anr'], wts['s5_ani'], wts['s5_d'], wts['w_glu'], wts['b_glu']]
    if prompt:
        state_spec = pl.BlockSpec((nsub, N_STATE), lambda c: (0, 0))
    else:
        state_spec = pl.BlockSpec((nsub, N_STATE), lambda c: (c, 0))
        state_in = pl.BlockSpec((None, nsub, N_STATE), lambda c: (layer, c, 0))
        in_specs += [state_in, state_in]
        args += [h0_re, h0_im]
    state_shape = jax.ShapeDtypeStruct((seqs, N_STATE), F32)
    return pl.pallas_call(
        functools.partial(_s5_body, lt=lt, nsub=nsub, prompt=prompt),
        grid=(steps,),
        in_specs=in_specs,
        out_specs=[pl.BlockSpec((SEQ_ROWS, SSM_WIDTH), lambda c: (c, 0)), state_spec, state_spec],
        out_shape=[jax.ShapeDtypeStruct((m, SSM_WIDTH), BF16), state_shape, state_shape],
        scratch_shapes=[pltpu.VMEM((SEQ_ROWS, SSM_WIDTH), F32)],
        compiler_params=_params(("arbitrary",)),
        name="s5_prompt" if prompt else "s5_sample",
    )(*args)


def _gla_body(*refs, lt, nsub, prompt):
    if prompt:
        (x_ref, q_ref, k_ref, v_ref, gz_ref, wal_ref, bal_ref, wa2_ref, ba_ref, ng_ref, tri_ref,
         o_ref, s_ref, o_scr) = refs
        s0_ref = s_ref

        @pl.when(pl.program_id(0) == 0)
        def _():
            s_ref[...] = jnp.zeros_like(s_ref)
    else:
        (x_ref, q_ref, k_ref, v_ref, gz_ref, wal_ref, bal_ref, wa2_ref, ba_ref, ng_ref, tri_ref,
         s0_ref, _, o_ref, s_ref, o_scr) = refs
    tri = tri_ref[...]
    a_low = lax.dot_general(x_ref[...], wal_ref[...], _NT, preferred_element_type=F32) + bal_ref[...]
    logit = jnp.dot(a_low.astype(BF16), wa2_ref[...], preferred_element_type=F32) + ba_ref[...]
    log_a = jax.nn.log_sigmoid(logit) * (1.0 / GLA_TAU)
    hi = log_a.astype(BF16)
    lo = (log_a - hi.astype(F32)).astype(BF16)
    cum = (jnp.dot(tri, hi, preferred_element_type=F32) + jnp.dot(tri, lo, preferred_element_type=F32))
    q = q_ref[...].astype(F32) * (GLA_DK ** -0.5)
    k = k_ref[...].astype(F32)
    qd = (q * jnp.exp(cum)).astype(BF16)
    kd = (k * jnp.exp(-cum)).astype(BF16)
    cum3 = cum.reshape(nsub, lt, GLA_KEY_WIDTH)
    last = cum3[:, lt - 1:lt, :]
    kk = (k.reshape(nsub, lt, GLA_KEY_WIDTH) * jnp.exp(last - cum3)).astype(BF16)
    e_last = jnp.exp(last)
    ri = lax.broadcasted_iota(jnp.int32, (lt, lt), 0)
    ci = lax.broadcasted_iota(jnp.int32, (lt, lt), 1)
    causal = ri >= ci
    combos = [(s, h) for s in range(nsub) for h in range(GLA_HEADS)]
    rs = lambda s: slice(s * lt, (s + 1) * lt)
    ks = lambda h: slice(h * GLA_DK, (h + 1) * GLA_DK)
    vs = lambda h: slice(h * GLA_DV, (h + 1) * GLA_DV)
    atts = []
    for s, h in combos:
        att = lax.dot_general(qd[rs(s), ks(h)], kd[rs(s), ks(h)], _NT, preferred_element_type=F32)
        atts.append(jnp.where(causal, att, 0.0).astype(BF16))
    for (s, h), att in zip(combos, atts):
        o_scr[rs(s), vs(h)] = (jnp.dot(att, v_ref[rs(s), vs(h)], preferred_element_type=F32)
                               + jnp.dot(qd[rs(s), ks(h)], s0_ref[s, h].astype(BF16),
                                         preferred_element_type=F32))
    for s, h in combos:
        kv = lax.dot_general(kk[s, :, ks(h)], v_ref[rs(s), vs(h)], (((0,), (0,)), ((), ())),
                             preferred_element_type=F32)
        dec = jnp.transpose(jnp.broadcast_to(e_last[s, :, ks(h)], (GLA_DK, GLA_DK)))
        s_ref[s, h] = jnp.concatenate([dec, dec], axis=1) * s0_ref[s, h] + kv
    for h in range(GLA_HEADS):
        o = o_scr[:, vs(h)]
        mu = jnp.mean(o, axis=-1, keepdims=True)
        var = jnp.mean(jnp.square(o - mu), axis=-1, keepdims=True)
        o = (o - mu) * lax.rsqrt(var + LN_EPS) * ng_ref[:, vs(h)]
        o_ref[:, vs(h)] = (o * gz_ref[:, vs(h)].astype(F32)).astype(o_ref.dtype)


def _gla_branch(layer, xb, proj, wts, s0, s_stack, *, lt, prompt):
    m = proj.shape[0]
    rows = SEQ_ROWS if prompt else GLA_SAMPLE_ROWS
    nsub = rows // lt
    steps = m // rows
    tri = _block_tri(rows, lt)
    col = lambda off, w: pl.BlockSpec((rows, w), lambda c: (c, off // w))
    in_specs = [pl.BlockSpec((rows, D_MODEL), lambda c: (c, 0)),
                col(COL_Q, GLA_KEY_WIDTH), col(COL_K, GLA_KEY_WIDTH), col(COL_V, GLA_VAL_WIDTH),
                col(COL_GLA_Z, GLA_VAL_WIDTH),
                _layer_spec(layer, (ALOW_PAD, D_MODEL)), _layer_spec(layer, (1, ALOW_PAD)),
                _layer_spec(layer, (ALOW_PAD, GLA_KEY_WIDTH)), _layer_spec(layer, (1, GLA_KEY_WIDTH)),
                _layer_spec(layer, (1, GLA_VAL_WIDTH)), _const_spec((rows, rows))]
    args = [xb, proj, proj, proj, proj, wts['w_alow'], wts['b_alow'], wts['gla_w_a2'], wts['gla_b_a'],
            wts['gla_norm_g'], tri]
    state_dims = (GLA_HEADS, GLA_DK, GLA_DV)
    aliases = {}
    if prompt:
        state_spec = pl.BlockSpec((nsub,) + state_dims, lambda c: (0, 0, 0, 0))
        state_shape = jax.ShapeDtypeStruct((nsub,) + state_dims, F32)
    else:
        state_spec = pl.BlockSpec((None, nsub) + state_dims, lambda c: (layer, c, 0, 0, 0))
        state_shape = jax.ShapeDtypeStruct(s0.shape, F32)
        in_specs += [state_spec]
        args += [s0]
        if s_stack is not None:
            in_specs += [pl.BlockSpec(memory_space=pl.ANY)]
            args += [s_stack]
            aliases = {len(args) - 1: 1}
    body = functools.partial(_gla_body, lt=lt, nsub=nsub, prompt=prompt)
    if not prompt and s_stack is None:
        inner = body
        body = lambda *refs: inner(*refs[:12], None, *refs[12:])
    return pl.pallas_call(
        body,
        grid=(steps,),
        in_specs=in_specs,
        out_specs=[pl.BlockSpec((rows, GLA_VAL_WIDTH), lambda c: (c, 0)), state_spec],
        out_shape=[jax.ShapeDtypeStruct((m, GLA_VAL_WIDTH), BF16), state_shape],
        scratch_shapes=[pltpu.VMEM((rows, GLA_VAL_WIDTH), F32)],
        input_output_aliases=aliases,
        compiler_params=_params(("arbitrary",)),
        name="gla_prompt" if prompt else "gla_sample",
    )(*args)


def _conv_body(*refs, lt, nsub, prompt):
    if prompt:
        (ca_ref, scb_ref, cz_ref, w_ref, b_ref, lng_ref, lnb_ref, o_ref, cache_out_ref, x_scr, y_scr) = refs

        @pl.when(pl.program_id(0) == 0)
        def _():
            x_scr[:, 0:CONV_HIST, :] = jnp.zeros((nsub, CONV_HIST, CONV_WIDTH), F32)
    else:
        (ca_ref, scb_ref, cz_ref, w_ref, b_ref, lng_ref, lnb_ref, cache_ref, _,
         o_ref, cache_out_ref, x_scr, y_scr) = refs
    pad = CONV_HIST - (CONV_K - 1)
    cw = 2 * LANES if lt > SUBLANES else CONV_WIDTH
    taps_by_shift = [[j for j in range(CONV_K) if (j + pad) % SUBLANES == sh] for sh in range(SUBLANES)]

    def per_seq(s, carry):
        r0 = pl.multiple_of(s * lt, SUBLANES)
        g = ca_ref[pl.ds(r0, lt), :].astype(F32) * scb_ref[pl.ds(r0, lt), :].astype(F32)
        x_scr[s, CONV_HIST:CONV_HIST + lt, :] = g
        if not prompt:
            x_scr[s, pad:CONV_HIST, :] = cache_ref[s]
        for c0 in range(0, CONV_WIDTH, cw):
            acc = None
            for sh, taps in enumerate(taps_by_shift):
                span = max(taps) + pad - sh + lt
                xs = x_scr[s, pl.ds(sh, span), c0:c0 + cw]
                part = None
                for j in taps:
                    base = j + pad - sh
                    term = w_ref[j:j + 1, c0:c0 + cw] * xs[base:base + lt, :]
                    part = term if part is None else part + term
                acc = part if acc is None else acc + part
            y_scr[pl.ds(r0, lt), c0:c0 + cw] = acc
        cache_out_ref[s] = x_scr[s, lt + pad:lt + CONV_HIST, :]
        if prompt:
            x_scr[s, 0:CONV_HIST, :] = x_scr[s, lt:lt + CONV_HIST, :]
        return carry

    lax.fori_loop(0, nsub, per_seq, 0)
    y = y_scr[...] + b_ref[...]
    mu = jnp.mean(y, axis=-1, keepdims=True)
    var = jnp.mean(jnp.square(y - mu), axis=-1, keepdims=True)
    y = (y - mu) * lax.rsqrt(var + LN_EPS) * lng_ref[...] + lnb_ref[...]
    o_ref[...] = (jax.nn.silu(y) * cz_ref[...].astype(F32)).astype(o_ref.dtype)


def _conv_branch(layer, proj, wts, cache, cache_stack, *, lt, prompt):
    m = proj.shape[0]
    nsub = SEQ_ROWS // lt
    steps = m // SEQ_ROWS
    col = lambda off: pl.BlockSpec((SEQ_ROWS, CONV_WIDTH), lambda c: (c, off // CONV_WIDTH))
    in_specs = [col(COL_CA), col(COL_CB), col(COL_CZ),
                _layer_spec(layer, (CONV_K, CONV_WIDTH)), _layer_spec(layer, (1, CONV_WIDTH)),
                _layer_spec(layer, (1, CONV_WIDTH)), _layer_spec(layer, (1, CONV_WIDTH))]
    args = [proj, proj, proj, wts['conv_w'], wts['conv_b'], wts['conv_ln_g'], wts['conv_ln_b']]
    aliases = {}
    if prompt:
        cache_spec = pl.BlockSpec((nsub, CONV_K - 1, CONV_WIDTH), lambda c: (0, 0, 0))
        cache_shape = jax.ShapeDtypeStruct((nsub, CONV_K - 1, CONV_WIDTH), F32)
    else:
        cache_spec = pl.BlockSpec((None, nsub, CONV_K - 1, CONV_WIDTH), lambda c: (layer, c, 0, 0))
        cache_shape = jax.ShapeDtypeStruct(cache.shape, F32)
        in_specs += [cache_spec]
        args += [cache]
        if cache_stack is not None:
            in_specs += [pl.BlockSpec(memory_space=pl.ANY)]
            args += [cache_stack]
            aliases = {len(args) - 1: 1}
    body = functools.partial(_conv_body, lt=lt, nsub=nsub, prompt=prompt)
    if not prompt and cache_stack is None:
        inner = body
        body = lambda *refs: inner(*refs[:8], None, *refs[8:])
    return pl.pallas_call(
        body,
        grid=(steps,),
        in_specs=in_specs,
        out_specs=[pl.BlockSpec((SEQ_ROWS, CONV_WIDTH), lambda c: (c, 0)), cache_spec],
        out_shape=[jax.ShapeDtypeStruct((m, CONV_WIDTH), BF16), cache_shape],
        scratch_shapes=[pltpu.VMEM((nsub, CONV_HIST + lt, CONV_WIDTH), F32),
                        pltpu.VMEM((SEQ_ROWS, CONV_WIDTH), F32)],
        input_output_aliases=aliases,
        compiler_params=_params(("arbitrary",)),
        name="conv_prompt" if prompt else "conv_sample",
    )(*args)


def _merge_body(a1_ref, a2_ref, a3_ref, g1_ref, g2_ref, g3_ref, p1_ref, p2_ref, p3_ref, o_ref):
    m = g1_ref[...].astype(F32) * jnp.dot(a1_ref[...], p1_ref[...], preferred_element_type=F32)
    m = m + g2_ref[...].astype(F32) * jnp.dot(a2_ref[...], p2_ref[...], preferred_element_type=F32)
    m = m + g3_ref[...].astype(F32) * jnp.dot(a3_ref[...], p3_ref[...], preferred_element_type=F32)
    o_ref[...] = m.astype(o_ref.dtype)


def _merge(layer, a_s5, a_gla, a_conv, proj, wts):
    m = a_s5.shape[0]
    rows = min(MERGE_ROWS, m)
    act = pl.BlockSpec((rows, SSM_WIDTH), lambda i: (i, 0))
    gate = lambda n: pl.BlockSpec((rows, D_MODEL), lambda i: (i, COL_GATES // D_MODEL + n))
    wspec = _layer_spec(layer, (SSM_WIDTH, D_MODEL))
    return pl.pallas_call(
        _merge_body,
        grid=(m // rows,),
        in_specs=[act, act, act, gate(0), gate(1), gate(2), wspec, wspec, wspec],
        out_specs=pl.BlockSpec((rows, D_MODEL), lambda i: (i, 0)),
        out_shape=jax.ShapeDtypeStruct((m, D_MODEL), BF16),
        compiler_params=_params(("arbitrary",)),
        name="merge",
    )(a_s5, a_gla, a_conv, proj, proj, proj, wts['p_s5'], wts['p_gla'], wts['p_conv'])


def _post_body(m_ref, x_ref, p_ref, wo_ref, wpg_ref, wpe_ref, g_ref, b_ref, xo_ref, xb_ref):
    for i, r in enumerate(range(0, x_ref.shape[0], SEQ_ROWS)):
        rs = slice(r, r + SEQ_ROWS)
        h = DN_ALPHA * x_ref[rs, :] + jnp.dot(m_ref[rs, :], wo_ref[...], preferred_element_type=F32)
        gate = jax.nn.sigmoid(jnp.dot(h.astype(BF16), wpg_ref[...], preferred_element_type=F32))
        if len(p_ref.shape) == 3:
            p = p_ref[:, i * CHUNK:(i + 1) * CHUNK, :].reshape(SEQ_ROWS, PLE_DIM).astype(BF16)
        else:
            p = p_ref[rs, :].astype(BF16)
        h = h + gate * jnp.dot(p, wpe_ref[...], preferred_element_type=F32)
        mu = jnp.mean(h, axis=-1, keepdims=True)
        var = jnp.mean(jnp.square(h - mu), axis=-1, keepdims=True)
        xn = (h - mu) * lax.rsqrt(var + LN_EPS) * g_ref[...] + b_ref[...]
        xo_ref[rs, :] = xn
        xb_ref[rs, :] = xn.astype(BF16)


def _post(layer, merged, x, p, wts, *, prompt):
    m = x.shape[0]
    rows = min(MERGE_ROWS, m)
    row = lambda w: pl.BlockSpec((rows, w), lambda i: (i, 0))
    if prompt:
        p_spec = pl.BlockSpec((None, p.shape[1], rows // p.shape[1], PLE_DIM), lambda i: (layer, 0, i, 0))
    else:
        p_spec = pl.BlockSpec((None, rows, PLE_DIM), lambda i: (layer, i, 0))
    return pl.pallas_call(
        _post_body,
        grid=(m // rows,),
        in_specs=[row(D_MODEL), row(D_MODEL), p_spec,
                  _layer_spec(layer, (D_MODEL, D_MODEL)), _layer_spec(layer, (D_MODEL, D_MODEL)),
                  _layer_spec(layer, (PLE_DIM, D_MODEL)),
                  _layer_spec(layer, (1, D_MODEL)), _layer_spec(layer, (1, D_MODEL))],
        out_specs=[row(D_MODEL), row(D_MODEL)],
        out_shape=[jax.ShapeDtypeStruct((m, D_MODEL), F32), jax.ShapeDtypeStruct((m, D_MODEL), BF16)],
        compiler_params=_params(("arbitrary",)),
        name="post",
    )(merged, x, p, wts['w_o'], wts['w_pg'], wts['w_pe'], wts['ln_g'], wts['ln_b'])


def _layer(layer, x, xb, p, wts, states, stacks, *, lt, prompt):
    s5_re, s5_im, gla_s, conv_buf = states
    gla_stack, conv_stack = stacks
    proj = _inproj(layer, xb, wts)
    a_s5, h_re, h_im = _s5_branch(layer, proj, wts, s5_re, s5_im, lt=lt, prompt=prompt)
    a_gla, gla_new = _gla_branch(layer, xb, proj, wts, gla_s, gla_stack, lt=lt, prompt=prompt)
    a_conv, conv_new = _conv_branch(layer, proj, wts, conv_buf, conv_stack, lt=lt, prompt=prompt)
    merged = _merge(layer, a_s5, a_gla, a_conv, proj, wts)
    x_new, xb_new = _post(layer, merged, x, p, wts, prompt=prompt)
    return x_new, xb_new, h_re, h_im, gla_new, conv_new


def _to_chunk_major(a, bsz, seq):
    w = a.shape[-1]
    a = a.reshape(bsz, seq // CHUNK, CHUNK, w)
    return jnp.transpose(a, (1, 0, 2, 3)).reshape(bsz * seq, w)


def _from_chunk_major(a, bsz, seq):
    w = a.shape[-1]
    a = a.reshape(seq // CHUNK, bsz, CHUNK, w)
    return jnp.transpose(a, (1, 0, 2, 3)).reshape(bsz, seq, w)


def kernel(x_prompt, x_sample, p_prompt, p_sample, state_s5_re, state_s5_im, state_gla, cache_conv, w_in, b_in, s5_a_re, s5_a_im, s5_log_dt, s5_b_re, s5_b_im, s5_c_re, s5_c_im, s5_d, w_glu, b_glu, gla_w_a2, gla_b_a, gla_norm_g, conv_w, conv_b, conv_ln_g, conv_ln_b, p_s5, p_gla, p_conv, w_o, w_pg, w_pe, ln_g, ln_b):
    bp, seq, _ = x_prompt.shape
    bs, dseq, _ = x_sample.shape
    depth = w_in.shape[0]
    assert seq % CHUNK == 0 and bp * CHUNK == SEQ_ROWS and (bs * dseq) % SEQ_ROWS == 0 and dseq == SUBLANES

    apr, api, anr, ani, bbr, bbi = _s5_prep(s5_a_re, s5_a_im, s5_log_dt, s5_b_re, s5_b_im)
    s5_bre, s5_cre = _s5_block_weights(bbr, s5_c_re)
    s5_bim, s5_cim = _s5_block_weights(bbi, s5_c_im)
    row = lambda a: a[:, None, :]
    alow = slice(*_IN_SRC['alow'])
    pad_rank = (0, ALOW_PAD - GLA_GATE_RANK)
    w_in_t = jnp.transpose(w_in, (0, 2, 1))
    wts = dict(
        w_in=_relayout_in_weights(w_in_t), b_in=_reorder_bias(b_in),
        w_alow=jnp.pad(w_in_t[:, alow, :], ((0, 0), pad_rank, (0, 0))).astype(BF16),
        b_alow=row(jnp.pad(b_in[:, alow], ((0, 0), pad_rank))),
        s5_bre=s5_bre, s5_bim=s5_bim, s5_cre=s5_cre, s5_cim=s5_cim,
        s5_apr=apr, s5_api=api, s5_anr=anr, s5_ani=ani,
        s5_d=row(s5_d), w_glu=w_glu.astype(BF16), b_glu=row(b_glu),
        gla_w_a2=jnp.pad(gla_w_a2, ((0, 0), pad_rank, (0, 0))).astype(BF16),
        gla_b_a=row(gla_b_a), gla_norm_g=row(gla_norm_g),
        conv_w=conv_w, conv_b=row(conv_b), conv_ln_g=row(conv_ln_g), conv_ln_b=row(conv_ln_b),
        p_s5=p_s5.astype(BF16), p_gla=p_gla.astype(BF16), p_conv=p_conv.astype(BF16),
        w_o=w_o.astype(BF16), w_pg=w_pg.astype(BF16), w_pe=w_pe.astype(BF16),
        ln_g=row(ln_g), ln_b=row(ln_b))

    hp = _to_chunk_major(x_prompt, bp, seq)
    hs = x_sample.reshape(bs * dseq, D_MODEL)
    hpb, hsb = hp.astype(BF16), hs.astype(BF16)
    ps = p_sample.reshape(depth, bs * dseq, PLE_DIM)
    sample_states = (state_s5_re.reshape(depth, bs, N_STATE), state_s5_im.reshape(depth, bs, N_STATE),
                     state_gla, cache_conv)
    pr_re, pr_im, pr_gla, pr_conv, sm_re, sm_im = [], [], [], [], [], []
    gla_stack = conv_stack = None
    for i in range(depth):
        hp, hpb, a, b, c, d = _layer(i, hp, hpb, p_prompt, wts, (None,) * 4, (None, None), lt=CHUNK, prompt=True)
        pr_re.append(a); pr_im.append(b); pr_gla.append(c); pr_conv.append(d)
        hs, hsb, a, b, gla_stack, conv_stack = _layer(i, hs, hsb, ps, wts, sample_states, (gla_stack, conv_stack),
                                                      lt=dseq, prompt=False)
        sm_re.append(a); sm_im.append(b)
    s5_shape = lambda a, n: jnp.stack(a).reshape(depth, n, SSM_GROUPS, SSM_STATE)
    return (_from_chunk_major(hp, bp, seq), hs.reshape(bs, dseq, D_MODEL),
            s5_shape(pr_re, bp), s5_shape(pr_im, bp), jnp.stack(pr_gla), jnp.stack(pr_conv),
            s5_shape(sm_re, bs), s5_shape(sm_im, bs), gla_stack, conv_stack)
```

```python
import functools

import numpy as np
import jax
import jax.numpy as jnp
from jax import lax
from jax.experimental import pallas as pl
from jax.experimental.pallas import tpu as pltpu

F32 = jnp.float32
BF16 = jnp.bfloat16

D_MODEL = 2048
PLE_DIM = 256
SSM_WIDTH = 1024
SSM_GROUP = 16
SSM_GROUPS = 64
SSM_STATE = 64
N_STATE = SSM_GROUPS * SSM_STATE
GLA_HEADS = 4
GLA_DK = 128
GLA_DV = 256
GLA_KEY_WIDTH = 512
GLA_VAL_WIDTH = 1024
GLA_GATE_RANK = 16
GLA_TAU = 16.0
CONV_WIDTH = 1024
CONV_K = 31
N_BRANCH = 3
DN_ALPHA = 8.0 ** 0.25
LN_EPS = 1e-5

SUBLANES = 8
LANES = 128
CONV_HIST = 32
CHUNK = 64
SEQ_ROWS = 256
GLA_SAMPLE_ROWS = 64
S5_SLAB = 1024
S5_USLAB = 256
N_SLAB = N_STATE // S5_SLAB
MERGE_ROWS = 512

VMEM_LIMIT = 56 * 1024 * 1024

PROJ_BN = 512
PROJ_BM = 4096
PROJ_SUB = 512
COL_GATES = 0
COL_S5_U = 6144
COL_V = 7168
COL_CA = 8192
COL_S5_Z = 9216
COL_GLA_Z = 10240
COL_CZ = 11264
COL_CB = 12288
COL_Q = 13312
COL_K = 13824
PROJ_COLS = 14336
ALOW_PAD = LANES

_IN_SIZES = [SSM_WIDTH, SSM_WIDTH, GLA_KEY_WIDTH, GLA_KEY_WIDTH, GLA_VAL_WIDTH, GLA_GATE_RANK, GLA_VAL_WIDTH,
             CONV_WIDTH, CONV_WIDTH, CONV_WIDTH, N_BRANCH * D_MODEL]
_IN_NAMES = ['s5_u', 's5_z', 'q', 'k', 'v', 'alow', 'gla_z', 'ca', 'cb', 'cz', 'gates']
_IN_OFFS = np.concatenate([[0], np.cumsum(_IN_SIZES)])
_IN_SRC = {n: (int(_IN_OFFS[i]), int(_IN_OFFS[i + 1])) for i, n in enumerate(_IN_NAMES)}
_IN_ORDER = ['gates', 's5_u', 'v', 'ca', 's5_z', 'gla_z', 'cz', 'cb', 'q', 'k']
N_IN = int(_IN_OFFS[-1])


def _layer_spec(layer, shape):
    zeros = (0,) * len(shape)
    return pl.BlockSpec((None,) + tuple(shape), lambda *_: (layer,) + zeros, pipeline_mode=pl.Buffered(1))


def _const_spec(shape):
    zeros = (0,) * len(shape)
    return pl.BlockSpec(tuple(shape), lambda *_: zeros, pipeline_mode=pl.Buffered(1))


def _params(sem):
    return pltpu.CompilerParams(dimension_semantics=sem, vmem_limit_bytes=VMEM_LIMIT)


def _in_block_table():
    src = []
    for name in _IN_ORDER:
        lo, hi = _IN_SRC[name]
        src += list(range(lo, hi, PROJ_BN))
    assert all(lo % GLA_GATE_RANK == 0 for lo in src)
    return jnp.asarray(np.asarray(src, np.int32) // GLA_GATE_RANK)


def _reorder_bias(b_in):
    parts = [b_in[:, _IN_SRC[n][0]:_IN_SRC[n][1]] for n in _IN_ORDER]
    return jnp.concatenate(parts, axis=-1)[:, None, :]


_NT = (((1,), (1,)), ((), ()))


def _inproj_body(tbl_ref, x_ref, w_ref, b_ref, o_ref, w_scr):
    j = pl.program_id(1)
    is_sig = (j < COL_S5_U // PROJ_BN) | ((j >= COL_CB // PROJ_BN) & (j < COL_Q // PROJ_BN))
    is_silu = (j >= COL_S5_Z // PROJ_BN) & (j < COL_CB // PROJ_BN)
    w_scr[...] = w_ref[0].astype(BF16)

    def run(act):
        for r in range(0, x_ref.shape[0], PROJ_SUB):
            acc = lax.dot_general(x_ref[r:r + PROJ_SUB, :], w_scr[...], _NT, preferred_element_type=F32)
            o_ref[r:r + PROJ_SUB, :] = act(acc + b_ref[...]).astype(o_ref.dtype)

    pl.when(is_sig)(lambda: run(jax.nn.sigmoid))
    pl.when(is_silu)(lambda: run(jax.nn.silu))
    pl.when(jnp.logical_not(is_sig | is_silu))(lambda: run(lambda a: a))


def _inproj(layer, x, wts):
    m = x.shape[0]
    bm = min(m, PROJ_BM)
    grid_spec = pltpu.PrefetchScalarGridSpec(
        num_scalar_prefetch=1,
        grid=(m // bm, PROJ_COLS // PROJ_BN),
        in_specs=[
            pl.BlockSpec((bm, D_MODEL), lambda i, j, tbl: (i, 0), pipeline_mode=pl.Buffered(1)),
            pl.BlockSpec((pl.Element(1), pl.Element(PROJ_BN), pl.Element(D_MODEL)),
                         lambda i, j, tbl: (layer, tbl[j] * GLA_GATE_RANK, 0)),
            pl.BlockSpec((None, 1, PROJ_BN), lambda i, j, tbl: (layer, 0, j))],
        out_specs=pl.BlockSpec((bm, PROJ_BN), lambda i, j, tbl: (i, j)),
        scratch_shapes=[pltpu.VMEM((PROJ_BN, D_MODEL), BF16)])
    return pl.pallas_call(
        _inproj_body,
        grid_spec=grid_spec,
        out_shape=jax.ShapeDtypeStruct((m, PROJ_COLS), BF16),
        compiler_params=_params(("arbitrary", "arbitrary")),
        name="inproj",
    )(wts['in_table'], x, wts['w_in'], wts['b_in'])


def _s5_prep_body(ar_ref, ai_ref, ldt_ref, bre_ref, bim_ref, apr_ref, api_ref, anr_ref, ani_ref, bbr_ref, bbi_ref):
    ar = ar_ref[0]
    ai = ai_ref[0]
    dt = jnp.exp(ldt_ref[0])
    xr = ar * dt
    xi = ai * dt
    t = lax.broadcasted_iota(jnp.int32, (CHUNK, N_STATE), 0).astype(F32)
    ang = t * xi
    cs = jnp.cos(ang)
    sn = jnp.sin(ang)
    mag = jnp.exp(t * xr)
    imag = jnp.exp(-(t * xr))
    apr_ref[0] = mag * cs
    api_ref[0] = mag * sn
    anr_ref[0] = imag * cs
    ani_ref[0] = -(imag * sn)
    e1 = jnp.exp(xr)
    nr = e1 * jnp.cos(xi) - 1.0
    ni = e1 * jnp.sin(xi)
    den = ar * ar + ai * ai
    cr = (nr * ar + ni * ai) / den
    ci = (ni * ar - nr * ai) / den
    bre = bre_ref[0]
    bim = bim_ref[0]
    bbr_ref[0] = cr * bre - ci * bim
    bbi_ref[0] = cr * bim + ci * bre


def _s5_prep(a_re, a_im, log_dt, b_re, b_im):
    depth = a_re.shape[0]
    ar = a_re.reshape(depth, 1, N_STATE)
    ai = a_im.reshape(depth, 1, N_STATE)
    ldt = jnp.broadcast_to(log_dt[:, :, None], (depth, SSM_GROUPS, SSM_STATE)).reshape(depth, 1, N_STATE)
    bre = jnp.transpose(b_re, (0, 3, 1, 2)).reshape(depth, SSM_GROUP, N_STATE)
    bim = jnp.transpose(b_im, (0, 3, 1, 2)).reshape(depth, SSM_GROUP, N_STATE)
    row = pl.BlockSpec((1, 1, N_STATE), lambda i: (i, 0, 0))
    grp = pl.BlockSpec((1, SSM_GROUP, N_STATE), lambda i: (i, 0, 0))
    tab = pl.BlockSpec((1, CHUNK, N_STATE), lambda i: (i, 0, 0))
    tab_shape = jax.ShapeDtypeStruct((depth, CHUNK, N_STATE), F32)
    grp_shape = jax.ShapeDtypeStruct((depth, SSM_GROUP, N_STATE), F32)
    return pl.pallas_call(
        _s5_prep_body,
        grid=(depth,),
        in_specs=[row, row, row, grp, grp],
        out_specs=[tab, tab, tab, tab, grp, grp],
        out_shape=[tab_shape, tab_shape, tab_shape, tab_shape, grp_shape, grp_shape],
        compiler_params=_params(("arbitrary",)),
        name="s5_prep",
    )(ar, ai, ldt, bre, bim)


def _s5_block_diag(a):
    depth = a.shape[0]
    a4 = jnp.transpose(a.reshape(depth, SSM_GROUP, N_SLAB, S5_SLAB), (0, 2, 1, 3))
    tiled = jnp.tile(a4, (1, 1, S5_USLAB // SSM_GROUP, 1))
    r = np.arange(S5_USLAB)[:, None] // SSM_GROUP
    c = np.arange(S5_SLAB)[None, :] // SSM_STATE
    return jnp.where(jnp.asarray(r == c), tiled, 0.0).astype(BF16)


def _s5_out_map(c):
    return jnp.transpose(c, (0, 2, 1, 3)).reshape(c.shape[0], SSM_GROUP, N_STATE)


def _cmul(ar, ai, br, bi):
    return ar * br - ai * bi, ar * bi + ai * br


def _s5_body(*refs, lt, nsub, prompt):
    if prompt:
        (u_ref, sz_ref, bre_ref, bim_ref, cre_ref, cim_ref, tri_ref, apr_ref, api_ref, anr_ref, ani_ref,
         d_ref, wglu_ref, bglu_ref, o_ref, hre_ref, him_ref, y_scr) = refs
        pre_ref, pim_ref = hre_ref, him_ref

        @pl.when(pl.program_id(0) == 0)
        def _():
            hre_ref[...] = jnp.zeros_like(hre_ref)
            him_ref[...] = jnp.zeros_like(him_ref)
    else:
        (u_ref, sz_ref, bre_ref, bim_ref, cre_ref, cim_ref, tri_ref, apr_ref, api_ref, anr_ref, ani_ref,
         d_ref, wglu_ref, bglu_ref, pre_ref, pim_ref, o_ref, hre_ref, him_ref, y_scr) = refs
    rows = nsub * lt
    tri = tri_ref[...]
    for nt in range(N_SLAB):
        sl = slice(nt * S5_SLAB, (nt + 1) * S5_SLAB)
        u_nt = u_ref[:, nt * S5_USLAB:(nt + 1) * S5_USLAB]
        bu_re = jnp.dot(u_nt, bre_ref[nt], preferred_element_type=F32).reshape(nsub, lt, S5_SLAB)
        bu_im = jnp.dot(u_nt, bim_ref[nt], preferred_element_type=F32).reshape(nsub, lt, S5_SLAB)
        w_re, w_im = _cmul(bu_re, bu_im, anr_ref[0:lt, sl][None], ani_ref[0:lt, sl][None])
        w_re = w_re.reshape(rows, S5_SLAB).astype(BF16)
        w_im = w_im.reshape(rows, S5_SLAB).astype(BF16)
        acc_re = jnp.dot(tri, w_re, preferred_element_type=F32).reshape(nsub, lt, S5_SLAB)
        acc_im = jnp.dot(tri, w_im, preferred_element_type=F32).reshape(nsub, lt, S5_SLAB)
        c_re, c_im = _cmul(apr_ref[1:2, sl], api_ref[1:2, sl], pre_ref[:, sl], pim_ref[:, sl])
        acc_re = acc_re + c_re[:, None, :]
        acc_im = acc_im + c_im[:, None, :]
        h_re, h_im = _cmul(acc_re, acc_im, apr_ref[0:lt, sl][None], api_ref[0:lt, sl][None])
        hre_ref[:, sl] = h_re[:, lt - 1, :]
        him_ref[:, sl] = h_im[:, lt - 1, :]
        h_re = h_re.reshape(rows, S5_SLAB).astype(BF16)
        h_im = h_im.reshape(rows, S5_SLAB).astype(BF16)
        y_scr[:, nt * S5_USLAB:(nt + 1) * S5_USLAB] = (
            lax.dot_general(h_re, cre_ref[nt], _NT, preferred_element_type=F32)
            - lax.dot_general(h_im, cim_ref[nt], _NT, preferred_element_type=F32))
    y = y_scr[...] + d_ref[...] * u_ref[...].astype(F32)
    y = jax.nn.gelu(y)
    gate = jnp.dot(y.astype(BF16), wglu_ref[...], preferred_element_type=F32) + bglu_ref[...]
    y = y * jax.nn.sigmoid(gate)
    o_ref[...] = (y * sz_ref[...].astype(F32)).astype(o_ref.dtype)


def _block_tri(rows, lt):
    r = np.arange(rows)
    same = (r[:, None] // lt) == (r[None, :] // lt)
    return jnp.asarray(np.where(same & (r[:, None] >= r[None, :]), 1.0, 0.0), dtype=BF16)


def _s5_branch(layer, proj, wts, h0_re, h0_im, *, lt, prompt):
    m = proj.shape[0]
    nsub = SEQ_ROWS // lt
    steps = m // SEQ_ROWS
    seqs = nsub if prompt else m // lt
    tri = _block_tri(SEQ_ROWS, lt)
    col = lambda off, w: pl.BlockSpec((SEQ_ROWS, w), lambda c: (c, off // w))
    tab = _layer_spec(layer, (CHUNK, N_STATE))
    in_specs = [col(COL_S5_U, SSM_WIDTH), col(COL_S5_Z, SSM_WIDTH),
                _layer_spec(layer, (N_SLAB, S5_USLAB, S5_SLAB)), _layer_spec(layer, (N_SLAB, S5_USLAB, S5_SLAB)),
                _layer_spec(layer, (N_SLAB, S5_USLAB, S5_SLAB)), _layer_spec(layer, (N_SLAB, S5_USLAB, S5_SLAB)),
                _const_spec((SEQ_ROWS, SEQ_ROWS)), tab, tab, tab, tab,
                _layer_spec(layer, (1, SSM_WIDTH)), _layer_spec(layer, (SSM_WIDTH, SSM_WIDTH)),
                _layer_spec(layer, (1, SSM_WIDTH))]
    args = [proj, proj, wts['s5_bre'], wts['s5_bim'], wts['s5_cre'], wts['s5_cim'], tri,
            wts['s5_apr'], wts['s5_api'], wts['s5_anr'], wts['s5_ani'], wts['s5_d'], wts['w_glu'], wts['b_glu']]
    if prompt:
        state_spec = pl.BlockSpec((nsub, N_STATE), lambda c: (0, 0))
    else:
        state_spec = pl.BlockSpec((nsub, N_STATE), lambda c: (c, 0))
        state_in = pl.BlockSpec((None, nsub, N_STATE), lambda c: (layer, c, 0))
        in_specs += [state_in, state_in]
        args += [h0_re, h0_im]
    state_shape = jax.ShapeDtypeStruct((seqs, N_STATE), F32)
    return pl.pallas_call(
        functools.partial(_s5_body, lt=lt, nsub=nsub, prompt=prompt),
        grid=(steps,),
        in_specs=in_specs,
        out_specs=[pl.BlockSpec((SEQ_ROWS, SSM_WIDTH), lambda c: (c, 0)), state_spec, state_spec],
        out_shape=[jax.ShapeDtypeStruct((m, SSM_WIDTH), BF16), state_shape, state_shape],
        scratch_shapes=[pltpu.VMEM((SEQ_ROWS, SSM_WIDTH), F32)],
        compiler_params=_params(("arbitrary",)),
        name="s5_prompt" if prompt else "s5_sample",
    )(*args)


def _gla_body(*refs, lt, nsub, prompt):
    if prompt:
        (x_ref, q_ref, k_ref, v_ref, gz_ref, wal_ref, bal_ref, wa2_ref, ba_ref, ng_ref, tri_ref,
         o_ref, s_ref, o_scr) = refs
        s0_ref = s_ref

        @pl.when(pl.program_id(0) == 0)
        def _():
            s_ref[...] = jnp.zeros_like(s_ref)
    else:
        (x_ref, q_ref, k_ref, v_ref, gz_ref, wal_ref, bal_ref, wa2_ref, ba_ref, ng_ref, tri_ref,
         s0_ref, _, o_ref, s_ref, o_scr) = refs
    tri = tri_ref[...]
    a_low = lax.dot_general(x_ref[...], wal_ref[...], _NT, preferred_element_type=F32) + bal_ref[...]
    logit = jnp.dot(a_low.astype(BF16), wa2_ref[...], preferred_element_type=F32) + ba_ref[...]
    log_a = jax.nn.log_sigmoid(logit) * (1.0 / GLA_TAU)
    hi = log_a.astype(BF16)
    lo = (log_a - hi.astype(F32)).astype(BF16)
    cum = (jnp.dot(tri, hi, preferred_element_type=F32) + jnp.dot(tri, lo, preferred_element_type=F32))
    q = q_ref[...].astype(F32) * (GLA_DK ** -0.5)
    k = k_ref[...].astype(F32)
    qd = (q * jnp.exp(cum)).astype(BF16)
    kd = (k * jnp.exp(-cum)).astype(BF16)
    cum3 = cum.reshape(nsub, lt, GLA_KEY_WIDTH)
    last = cum3[:, lt - 1:lt, :]
    kk = (k.reshape(nsub, lt, GLA_KEY_WIDTH) * jnp.exp(last - cum3)).astype(BF16)
    e_last = jnp.exp(last)
    ri = lax.broadcasted_iota(jnp.int32, (lt, lt), 0)
    ci = lax.broadcasted_iota(jnp.int32, (lt, lt), 1)
    causal = ri >= ci
    combos = [(s, h) for s in range(nsub) for h in range(GLA_HEADS)]
    rs = lambda s: slice(s * lt, (s + 1) * lt)
    ks = lambda h: slice(h * GLA_DK, (h + 1) * GLA_DK)
    vs = lambda h: slice(h * GLA_DV, (h + 1) * GLA_DV)
    atts = []
    for s, h in combos:
        att = lax.dot_general(qd[rs(s), ks(h)], kd[rs(s), ks(h)], _NT, preferred_element_type=F32)
        atts.append(jnp.where(causal, att, 0.0).astype(BF16))
    for (s, h), att in zip(combos, atts):
        o_scr[rs(s), vs(h)] = (jnp.dot(att, v_ref[rs(s), vs(h)], preferred_element_type=F32)
                               + jnp.dot(qd[rs(s), ks(h)], s0_ref[s, h].astype(BF16),
                                         preferred_element_type=F32))
    for s, h in combos:
        kv = lax.dot_general(kk[s, :, ks(h)], v_ref[rs(s), vs(h)], (((0,), (0,)), ((), ())),
                             preferred_element_type=F32)
        dec = jnp.transpose(jnp.broadcast_to(e_last[s, :, ks(h)], (GLA_DK, GLA_DK)))
        s_ref[s, h] = jnp.concatenate([dec, dec], axis=1) * s0_ref[s, h] + kv
    for h in range(GLA_HEADS):
        o = o_scr[:, vs(h)]
        mu = jnp.mean(o, axis=-1, keepdims=True)
        var = jnp.mean(jnp.square(o - mu), axis=-1, keepdims=True)
        o = (o - mu) * lax.rsqrt(var + LN_EPS) * ng_ref[:, vs(h)]
        o_ref[:, vs(h)] = (o * gz_ref[:, vs(h)].astype(F32)).astype(o_ref.dtype)


def _gla_branch(layer, xb, proj, wts, s0, s_stack, *, lt, prompt):
    m = proj.shape[0]
    rows = SEQ_ROWS if prompt else GLA_SAMPLE_ROWS
    nsub = rows // lt
    steps = m // rows
    tri = _block_tri(rows, lt)
    col = lambda off, w: pl.BlockSpec((rows, w), lambda c: (c, off // w))
    in_specs = [pl.BlockSpec((rows, D_MODEL), lambda c: (c, 0)),
                col(COL_Q, GLA_KEY_WIDTH), col(COL_K, GLA_KEY_WIDTH), col(COL_V, GLA_VAL_WIDTH),
                col(COL_GLA_Z, GLA_VAL_WIDTH),
                _layer_spec(layer, (ALOW_PAD, D_MODEL)), _layer_spec(layer, (1, ALOW_PAD)),
                _layer_spec(layer, (ALOW_PAD, GLA_KEY_WIDTH)), _layer_spec(layer, (1, GLA_KEY_WIDTH)),
                _layer_spec(layer, (1, GLA_VAL_WIDTH)), _const_spec((rows, rows))]
    args = [xb, proj, proj, proj, proj, wts['w_alow'], wts['b_alow'], wts['gla_w_a2'], wts['gla_b_a'],
            wts['gla_norm_g'], tri]
    state_dims = (GLA_HEADS, GLA_DK, GLA_DV)
    aliases = {}
    if prompt:
        state_spec = pl.BlockSpec((nsub,) + state_dims, lambda c: (0, 0, 0, 0))
        state_shape = jax.ShapeDtypeStruct((nsub,) + state_dims, F32)
    else:
        state_spec = pl.BlockSpec((None, nsub) + state_dims, lambda c: (layer, c, 0, 0, 0))
        state_shape = jax.ShapeDtypeStruct(s0.shape, F32)
        in_specs += [state_spec, pl.BlockSpec(memory_space=pl.ANY)]
        args += [s0, s_stack]
        aliases = {len(args) - 1: 1}
    return pl.pallas_call(
        functools.partial(_gla_body, lt=lt, nsub=nsub, prompt=prompt),
        grid=(steps,),
        in_specs=in_specs,
        out_specs=[pl.BlockSpec((rows, GLA_VAL_WIDTH), lambda c: (c, 0)), state_spec],
        out_shape=[jax.ShapeDtypeStruct((m, GLA_VAL_WIDTH), BF16), state_shape],
        scratch_shapes=[pltpu.VMEM((rows, GLA_VAL_WIDTH), F32)],
        input_output_aliases=aliases,
        compiler_params=_params(("arbitrary",)),
        name="gla_prompt" if prompt else "gla_sample",
    )(*args)


def _conv_body(*refs, lt, nsub, prompt):
    if prompt:
        (ca_ref, scb_ref, cz_ref, w_ref, b_ref, lng_ref, lnb_ref, o_ref, cache_out_ref, x_scr, y_scr) = refs

        @pl.when(pl.program_id(0) == 0)
        def _():
            x_scr[:, 0:CONV_HIST, :] = jnp.zeros((nsub, CONV_HIST, CONV_WIDTH), F32)
    else:
        (ca_ref, scb_ref, cz_ref, w_ref, b_ref, lng_ref, lnb_ref, cache_ref, _,
         o_ref, cache_out_ref, x_scr, y_scr) = refs
    pad = CONV_HIST - (CONV_K - 1)
    cw = 2 * LANES if lt > SUBLANES else CONV_WIDTH
    taps_by_shift = [[j for j in range(CONV_K) if (j + pad) % SUBLANES == sh] for sh in range(SUBLANES)]

    def per_seq(s, carry):
        r0 = pl.multiple_of(s * lt, SUBLANES)
        g = ca_ref[pl.ds(r0, lt), :].astype(F32) * scb_ref[pl.ds(r0, lt), :].astype(F32)
        x_scr[s, CONV_HIST:CONV_HIST + lt, :] = g
        if not prompt:
            x_scr[s, pad:CONV_HIST, :] = cache_ref[s]
        for c0 in range(0, CONV_WIDTH, cw):
            acc = None
            for sh, taps in enumerate(taps_by_shift):
                span = max(taps) + pad - sh + lt
                xs = x_scr[s, pl.ds(sh, span), c0:c0 + cw]
                part = None
                for j in taps:
                    base = j + pad - sh
                    term = w_ref[j:j + 1, c0:c0 + cw] * xs[base:base + lt, :]
                    part = term if part is None else part + term
                acc = part if acc is None else acc + part
            y_scr[pl.ds(r0, lt), c0:c0 + cw] = acc
        cache_out_ref[s] = x_scr[s, lt + pad:lt + CONV_HIST, :]
        if prompt:
            x_scr[s, 0:CONV_HIST, :] = x_scr[s, lt:lt + CONV_HIST, :]
        return carry

    lax.fori_loop(0, nsub, per_seq, 0)
    y = y_scr[...] + b_ref[...]
    mu = jnp.mean(y, axis=-1, keepdims=True)
    var = jnp.mean(jnp.square(y - mu), axis=-1, keepdims=True)
    y = (y - mu) * lax.rsqrt(var + LN_EPS) * lng_ref[...] + lnb_ref[...]
    o_ref[...] = (jax.nn.silu(y) * cz_ref[...].astype(F32)).astype(o_ref.dtype)


def _conv_branch(layer, proj, wts, cache, cache_stack, *, lt, prompt):
    m = proj.shape[0]
    nsub = SEQ_ROWS // lt
    steps = m // SEQ_ROWS
    col = lambda off: pl.BlockSpec((SEQ_ROWS, CONV_WIDTH), lambda c: (c, off // CONV_WIDTH))
    in_specs = [col(COL_CA), col(COL_CB), col(COL_CZ),
                _layer_spec(layer, (CONV_K, CONV_WIDTH)), _layer_spec(layer, (1, CONV_WIDTH)),
                _layer_spec(layer, (1, CONV_WIDTH)), _layer_spec(layer, (1, CONV_WIDTH))]
    args = [proj, proj, proj, wts['conv_w'], wts['conv_b'], wts['conv_ln_g'], wts['conv_ln_b']]
    aliases = {}
    if prompt:
        cache_spec = pl.BlockSpec((nsub, CONV_K - 1, CONV_WIDTH), lambda c: (0, 0, 0))
        cache_shape = jax.ShapeDtypeStruct((nsub, CONV_K - 1, CONV_WIDTH), F32)
    else:
        cache_spec = pl.BlockSpec((None, nsub, CONV_K - 1, CONV_WIDTH), lambda c: (layer, c, 0, 0))
        cache_shape = jax.ShapeDtypeStruct(cache.shape, F32)
        in_specs += [cache_spec, pl.BlockSpec(memory_space=pl.ANY)]
        args += [cache, cache_stack]
        aliases = {len(args) - 1: 1}
    return pl.pallas_call(
        functools.partial(_conv_body, lt=lt, nsub=nsub, prompt=prompt),
        grid=(steps,),
        in_specs=in_specs,
        out_specs=[pl.BlockSpec((SEQ_ROWS, CONV_WIDTH), lambda c: (c, 0)), cache_spec],
        out_shape=[jax.ShapeDtypeStruct((m, CONV_WIDTH), BF16), cache_shape],
        scratch_shapes=[pltpu.VMEM((nsub, CONV_HIST + lt, CONV_WIDTH), F32),
                        pltpu.VMEM((SEQ_ROWS, CONV_WIDTH), F32)],
        input_output_aliases=aliases,
        compiler_params=_params(("arbitrary",)),
        name="conv_prompt" if prompt else "conv_sample",
    )(*args)


def _merge_body(a1_ref, a2_ref, a3_ref, g1_ref, g2_ref, g3_ref, p1_ref, p2_ref, p3_ref, o_ref):
    m = g1_ref[...].astype(F32) * jnp.dot(a1_ref[...], p1_ref[...], preferred_element_type=F32)
    m = m + g2_ref[...].astype(F32) * jnp.dot(a2_ref[...], p2_ref[...], preferred_element_type=F32)
    m = m + g3_ref[...].astype(F32) * jnp.dot(a3_ref[...], p3_ref[...], preferred_element_type=F32)
    o_ref[...] = m.astype(o_ref.dtype)


def _merge(layer, a_s5, a_gla, a_conv, proj, wts):
    m = a_s5.shape[0]
    rows = min(MERGE_ROWS, m)
    act = pl.BlockSpec((rows, SSM_WIDTH), lambda i: (i, 0))
    gate = lambda n: pl.BlockSpec((rows, D_MODEL), lambda i: (i, COL_GATES // D_MODEL + n))
    wspec = _layer_spec(layer, (SSM_WIDTH, D_MODEL))
    return pl.pallas_call(
        _merge_body,
        grid=(m // rows,),
        in_specs=[act, act, act, gate(0), gate(1), gate(2), wspec, wspec, wspec],
        out_specs=pl.BlockSpec((rows, D_MODEL), lambda i: (i, 0)),
        out_shape=jax.ShapeDtypeStruct((m, D_MODEL), BF16),
        compiler_params=_params(("arbitrary",)),
        name="merge",
    )(a_s5, a_gla, a_conv, proj, proj, proj, wts['p_s5'], wts['p_gla'], wts['p_conv'])


def _post_body(m_ref, x_ref, p_ref, wo_ref, wpg_ref, wpe_ref, g_ref, b_ref, xo_ref, xb_ref):
    for i, r in enumerate(range(0, x_ref.shape[0], SEQ_ROWS)):
        rs = slice(r, r + SEQ_ROWS)
        h = DN_ALPHA * x_ref[rs, :] + jnp.dot(m_ref[rs, :], wo_ref[...], preferred_element_type=F32)
        gate = jax.nn.sigmoid(jnp.dot(h.astype(BF16), wpg_ref[...], preferred_element_type=F32))
        if len(p_ref.shape) == 3:
            p = p_ref[:, i * CHUNK:(i + 1) * CHUNK, :].reshape(SEQ_ROWS, PLE_DIM).astype(BF16)
        else:
            p = p_ref[rs, :].astype(BF16)
        h = h + gate * jnp.dot(p, wpe_ref[...], preferred_element_type=F32)
        mu = jnp.mean(h, axis=-1, keepdims=True)
        var = jnp.mean(jnp.square(h - mu), axis=-1, keepdims=True)
        xn = (h - mu) * lax.rsqrt(var + LN_EPS) * g_ref[...] + b_ref[...]
        xo_ref[rs, :] = xn
        xb_ref[rs, :] = xn.astype(BF16)


def _post(layer, merged, x, p, wts, *, prompt):
    m = x.shape[0]
    rows = min(MERGE_ROWS, m)
    row = lambda w: pl.BlockSpec((rows, w), lambda i: (i, 0))
    if prompt:
        p_spec = pl.BlockSpec((None, p.shape[1], rows // p.shape[1], PLE_DIM), lambda i: (layer, 0, i, 0))
    else:
        p_spec = pl.BlockSpec((None, rows, PLE_DIM), lambda i: (layer, i, 0))
    return pl.pallas_call(
        _post_body,
        grid=(m // rows,),
        in_specs=[row(D_MODEL), row(D_MODEL), p_spec,
                  _layer_spec(layer, (D_MODEL, D_MODEL)), _layer_spec(layer, (D_MODEL, D_MODEL)),
                  _layer_spec(layer, (PLE_DIM, D_MODEL)),
                  _layer_spec(layer, (1, D_MODEL)), _layer_spec(layer, (1, D_MODEL))],
        out_specs=[row(D_MODEL), row(D_MODEL)],
        out_shape=[jax.ShapeDtypeStruct((m, D_MODEL), F32), jax.ShapeDtypeStruct((m, D_MODEL), BF16)],
        compiler_params=_params(("arbitrary",)),
        name="post",
    )(merged, x, p, wts['w_o'], wts['w_pg'], wts['w_pe'], wts['ln_g'], wts['ln_b'])


def _layer(layer, x, xb, p, wts, states, stacks, *, lt, prompt):
    s5_re, s5_im, gla_s, conv_buf = states
    gla_stack, conv_stack = stacks
    proj = _inproj(layer, xb, wts)
    a_s5, h_re, h_im = _s5_branch(layer, proj, wts, s5_re, s5_im, lt=lt, prompt=prompt)
    a_gla, gla_new = _gla_branch(layer, xb, proj, wts, gla_s, gla_stack, lt=lt, prompt=prompt)
    a_conv, conv_new = _conv_branch(layer, proj, wts, conv_buf, conv_stack, lt=lt, prompt=prompt)
    merged = _merge(layer, a_s5, a_gla, a_conv, proj, wts)
    x_new, xb_new = _post(layer, merged, x, p, wts, prompt=prompt)
    return x_new, xb_new, h_re, h_im, gla_new, conv_new


def _to_chunk_major(a, bsz, seq):
    w = a.shape[-1]
    a = a.reshape(bsz, seq // CHUNK, CHUNK, w)
    return jnp.transpose(a, (1, 0, 2, 3)).reshape(bsz * seq, w)


def _from_chunk_major(a, bsz, seq):
    w = a.shape[-1]
    a = a.reshape(seq // CHUNK, bsz, CHUNK, w)
    return jnp.transpose(a, (1, 0, 2, 3)).reshape(bsz, seq, w)


def kernel(x_prompt, x_sample, p_prompt, p_sample, state_s5_re, state_s5_im, state_gla, cache_conv, w_in, b_in, s5_a_re, s5_a_im, s5_log_dt, s5_b_re, s5_b_im, s5_c_re, s5_c_im, s5_d, w_glu, b_glu, gla_w_a2, gla_b_a, gla_norm_g, conv_w, conv_b, conv_ln_g, conv_ln_b, p_s5, p_gla, p_conv, w_o, w_pg, w_pe, ln_g, ln_b):
    bp, seq, _ = x_prompt.shape
    bs, dseq, _ = x_sample.shape
    depth = w_in.shape[0]
    assert seq % CHUNK == 0 and bp * CHUNK == SEQ_ROWS and (bs * dseq) % SEQ_ROWS == 0 and dseq == SUBLANES

    apr, api, anr, ani, bbr, bbi = _s5_prep(s5_a_re, s5_a_im, s5_log_dt, s5_b_re, s5_b_im)
    s5_bre, s5_bim = _s5_block_diag(bbr), _s5_block_diag(bbi)
    s5_cre, s5_cim = _s5_block_diag(_s5_out_map(s5_c_re)), _s5_block_diag(_s5_out_map(s5_c_im))
    row = lambda a: a[:, None, :]
    alow = slice(*_IN_SRC['alow'])
    pad_rank = (0, ALOW_PAD - GLA_GATE_RANK)
    w_in_t = jnp.transpose(w_in, (0, 2, 1))
    wts = dict(
        w_in=w_in_t, in_table=_in_block_table(), b_in=_reorder_bias(b_in),
        w_alow=jnp.pad(w_in_t[:, alow, :], ((0, 0), pad_rank, (0, 0))).astype(BF16),
        b_alow=row(jnp.pad(b_in[:, alow], ((0, 0), pad_rank))),
        s5_bre=s5_bre, s5_bim=s5_bim, s5_cre=s5_cre, s5_cim=s5_cim,
        s5_apr=apr, s5_api=api, s5_anr=anr, s5_ani=ani,
        s5_d=row(s5_d), w_glu=w_glu.astype(BF16), b_glu=row(b_glu),
        gla_w_a2=jnp.pad(gla_w_a2, ((0, 0), pad_rank, (0, 0))).astype(BF16),
        gla_b_a=row(gla_b_a), gla_norm_g=row(gla_norm_g),
        conv_w=conv_w, conv_b=row(conv_b), conv_ln_g=row(conv_ln_g), conv_ln_b=row(conv_ln_b),
        p_s5=p_s5.astype(BF16), p_gla=p_gla.astype(BF16), p_conv=p_conv.astype(BF16),
        w_o=w_o.astype(BF16), w_pg=w_pg.astype(BF16), w_pe=w_pe.astype(BF16),
        ln_g=row(ln_g), ln_b=row(ln_b))

    hp = _to_chunk_major(x_prompt, bp, seq)
    hs = x_sample.reshape(bs * dseq, D_MODEL)
    hpb, hsb = hp.astype(BF16), hs.astype(BF16)
    ps = p_sample.reshape(depth, bs * dseq, PLE_DIM)
    sample_states = (state_s5_re.reshape(depth, bs, N_STATE), state_s5_im.reshape(depth, bs, N_STATE),
                     state_gla, cache_conv)
    pr_re, pr_im, pr_gla, pr_conv, sm_re, sm_im = [], [], [], [], [], []
    gla_stack, conv_stack = jnp.zeros(state_gla.shape, F32), jnp.zeros(cache_conv.shape, F32)
    for i in range(depth):
        hp, hpb, a, b, c, d = _layer(i, hp, hpb, p_prompt, wts, (None,) * 4, (None, None), lt=CHUNK, prompt=True)
        pr_re.append(a); pr_im.append(b); pr_gla.append(c); pr_conv.append(d)
        hs, hsb, a, b, gla_stack, conv_stack = _layer(i, hs, hsb, ps, wts, sample_states, (gla_stack, conv_stack),
                                                      lt=dseq, prompt=False)
        sm_re.append(a); sm_im.append(b)
    s5_shape = lambda a, n: jnp.stack(a).reshape(depth, n, SSM_GROUPS, SSM_STATE)
    return (_from_chunk_major(hp, bp, seq), hs.reshape(bs, dseq, D_MODEL),
            s5_shape(pr_re, bp), s5_shape(pr_im, bp), jnp.stack(pr_gla), jnp.stack(pr_conv),
            s5_shape(sm_re, bs), s5_shape(sm_im, bs), gla_stack, conv_stack)
```

```python
import functools

import numpy as np
import jax
import jax.numpy as jnp
from jax import lax
from jax.experimental import pallas as pl
from jax.experimental.pallas import tpu as pltpu

F32 = jnp.float32
BF16 = jnp.bfloat16

D_MODEL = 2048
PLE_DIM = 256
SSM_WIDTH = 1024
SSM_GROUP = 16
SSM_GROUPS = 64
SSM_STATE = 64
N_STATE = SSM_GROUPS * SSM_STATE
GLA_HEADS = 4
GLA_DK = 128
GLA_DV = 256
GLA_KEY_WIDTH = 512
GLA_VAL_WIDTH = 1024
GLA_GATE_RANK = 16
GLA_TAU = 16.0
CONV_WIDTH = 1024
CONV_K = 31
N_BRANCH = 3
DN_ALPHA = 8.0 ** 0.25
LN_EPS = 1e-5

SUBLANES = 8
LANES = 128
CONV_HIST = 32
CHUNK = 64
SEQ_ROWS = 256
GLA_SAMPLE_ROWS = 64
S5_SLAB = 1024
S5_USLAB = 256
N_SLAB = N_STATE // S5_SLAB
MERGE_ROWS = 512

VMEM_LIMIT = 56 * 1024 * 1024

PROJ_BN = 512
PROJ_BM = 4096
PROJ_SUB = 512
COL_GATES = 0
COL_S5_U = 6144
COL_V = 7168
COL_CA = 8192
COL_S5_Z = 9216
COL_GLA_Z = 10240
COL_CZ = 11264
COL_CB = 12288
COL_Q = 13312
COL_K = 13824
PROJ_COLS = 14336
ALOW_PAD = LANES

_IN_SIZES = [SSM_WIDTH, SSM_WIDTH, GLA_KEY_WIDTH, GLA_KEY_WIDTH, GLA_VAL_WIDTH, GLA_GATE_RANK, GLA_VAL_WIDTH,
             CONV_WIDTH, CONV_WIDTH, CONV_WIDTH, N_BRANCH * D_MODEL]
_IN_NAMES = ['s5_u', 's5_z', 'q', 'k', 'v', 'alow', 'gla_z', 'ca', 'cb', 'cz', 'gates']
_IN_OFFS = np.concatenate([[0], np.cumsum(_IN_SIZES)])
_IN_SRC = {n: (int(_IN_OFFS[i]), int(_IN_OFFS[i + 1])) for i, n in enumerate(_IN_NAMES)}
_IN_ORDER = ['gates', 's5_u', 'v', 'ca', 's5_z', 'gla_z', 'cz', 'cb', 'q', 'k']
N_IN = int(_IN_OFFS[-1])


def _layer_spec(layer, shape):
    zeros = (0,) * len(shape)
    return pl.BlockSpec((None,) + tuple(shape), lambda *_: (layer,) + zeros, pipeline_mode=pl.Buffered(1))


def _const_spec(shape):
    zeros = (0,) * len(shape)
    return pl.BlockSpec(tuple(shape), lambda *_: zeros, pipeline_mode=pl.Buffered(1))


def _params(sem):
    return pltpu.CompilerParams(dimension_semantics=sem, vmem_limit_bytes=VMEM_LIMIT)


def _in_block_table():
    src = []
    for name in _IN_ORDER:
        lo, hi = _IN_SRC[name]
        src += list(range(lo, hi, PROJ_BN))
    assert all(lo % GLA_GATE_RANK == 0 for lo in src)
    return jnp.asarray(np.asarray(src, np.int32) // GLA_GATE_RANK)


def _reorder_bias(b_in):
    parts = [b_in[:, _IN_SRC[n][0]:_IN_SRC[n][1]] for n in _IN_ORDER]
    return jnp.concatenate(parts, axis=-1)[:, None, :]


_NT = (((1,), (1,)), ((), ()))


def _inproj_body(tbl_ref, x_ref, w_ref, b_ref, o_ref, w_scr):
    j = pl.program_id(1)
    is_sig = (j < COL_S5_U // PROJ_BN) | ((j >= COL_CB // PROJ_BN) & (j < COL_Q // PROJ_BN))
    is_silu = (j >= COL_S5_Z // PROJ_BN) & (j < COL_CB // PROJ_BN)
    w_scr[...] = w_ref[0].astype(BF16)

    def run(act):
        for r in range(0, x_ref.shape[0], PROJ_SUB):
            acc = lax.dot_general(x_ref[r:r + PROJ_SUB, :], w_scr[...], _NT, preferred_element_type=F32)
            o_ref[r:r + PROJ_SUB, :] = act(acc + b_ref[...]).astype(o_ref.dtype)

    pl.when(is_sig)(lambda: run(jax.nn.sigmoid))
    pl.when(is_silu)(lambda: run(jax.nn.silu))
    pl.when(jnp.logical_not(is_sig | is_silu))(lambda: run(lambda a: a))


def _inproj(layer, x, wts):
    m = x.shape[0]
    bm = min(m, PROJ_BM)
    grid_spec = pltpu.PrefetchScalarGridSpec(
        num_scalar_prefetch=1,
        grid=(m // bm, PROJ_COLS // PROJ_BN),
        in_specs=[
            pl.BlockSpec((bm, D_MODEL), lambda i, j, tbl: (i, 0), pipeline_mode=pl.Buffered(1)),
            pl.BlockSpec((pl.Element(1), pl.Element(PROJ_BN), pl.Element(D_MODEL)),
                         lambda i, j, tbl: (layer, tbl[j] * GLA_GATE_RANK, 0)),
            pl.BlockSpec((None, 1, PROJ_BN), lambda i, j, tbl: (layer, 0, j))],
        out_specs=pl.BlockSpec((bm, PROJ_BN), lambda i, j, tbl: (i, j)),
        scratch_shapes=[pltpu.VMEM((PROJ_BN, D_MODEL), BF16)])
    return pl.pallas_call(
        _inproj_body,
        grid_spec=grid_spec,
        out_shape=jax.ShapeDtypeStruct((m, PROJ_COLS), BF16),
        compiler_params=_params(("arbitrary", "arbitrary")),
        name="inproj",
    )(wts['in_table'], x, wts['w_in'], wts['b_in'])


def _s5_prep_body(ar_ref, ai_ref, ldt_ref, bre_ref, bim_ref, apr_ref, api_ref, anr_ref, ani_ref, bbr_ref, bbi_ref):
    ar = ar_ref[0]
    ai = ai_ref[0]
    dt = jnp.exp(ldt_ref[0])
    xr = ar * dt
    xi = ai * dt
    t = lax.broadcasted_iota(jnp.int32, (CHUNK, N_STATE), 0).astype(F32)
    ang = t * xi
    cs = jnp.cos(ang)
    sn = jnp.sin(ang)
    mag = jnp.exp(t * xr)
    imag = jnp.exp(-(t * xr))
    apr_ref[0] = mag * cs
    api_ref[0] = mag * sn
    anr_ref[0] = imag * cs
    ani_ref[0] = -(imag * sn)
    e1 = jnp.exp(xr)
    nr = e1 * jnp.cos(xi) - 1.0
    ni = e1 * jnp.sin(xi)
    den = ar * ar + ai * ai
    cr = (nr * ar + ni * ai) / den
    ci = (ni * ar - nr * ai) / den
    bre = bre_ref[0]
    bim = bim_ref[0]
    bbr_ref[0] = cr * bre - ci * bim
    bbi_ref[0] = cr * bim + ci * bre


def _s5_prep(a_re, a_im, log_dt, b_re, b_im):
    depth = a_re.shape[0]
    ar = a_re.reshape(depth, 1, N_STATE)
    ai = a_im.reshape(depth, 1, N_STATE)
    ldt = jnp.broadcast_to(log_dt[:, :, None], (depth, SSM_GROUPS, SSM_STATE)).reshape(depth, 1, N_STATE)
    bre = jnp.transpose(b_re, (0, 3, 1, 2)).reshape(depth, SSM_GROUP, N_STATE)
    bim = jnp.transpose(b_im, (0, 3, 1, 2)).reshape(depth, SSM_GROUP, N_STATE)
    row = pl.BlockSpec((1, 1, N_STATE), lambda i: (i, 0, 0))
    grp = pl.BlockSpec((1, SSM_GROUP, N_STATE), lambda i: (i, 0, 0))
    tab = pl.BlockSpec((1, CHUNK, N_STATE), lambda i: (i, 0, 0))
    tab_shape = jax.ShapeDtypeStruct((depth, CHUNK, N_STATE), F32)
    grp_shape = jax.ShapeDtypeStruct((depth, SSM_GROUP, N_STATE), F32)
    return pl.pallas_call(
        _s5_prep_body,
        grid=(depth,),
        in_specs=[row, row, row, grp, grp],
        out_specs=[tab, tab, tab, tab, grp, grp],
        out_shape=[tab_shape, tab_shape, tab_shape, tab_shape, grp_shape, grp_shape],
        compiler_params=_params(("arbitrary",)),
        name="s5_prep",
    )(ar, ai, ldt, bre, bim)


def _s5_block_diag(a):
    depth = a.shape[0]
    a4 = jnp.transpose(a.reshape(depth, SSM_GROUP, N_SLAB, S5_SLAB), (0, 2, 1, 3))
    tiled = jnp.tile(a4, (1, 1, S5_USLAB // SSM_GROUP, 1))
    r = np.arange(S5_USLAB)[:, None] // SSM_GROUP
    c = np.arange(S5_SLAB)[None, :] // SSM_STATE
    return jnp.where(jnp.asarray(r == c), tiled, 0.0).astype(BF16)


def _s5_out_map(c):
    return jnp.transpose(c, (0, 2, 1, 3)).reshape(c.shape[0], SSM_GROUP, N_STATE)


def _cmul(ar, ai, br, bi):
    return ar * br - ai * bi, ar * bi + ai * br


def _s5_body(*refs, lt, nsub, prompt):
    if prompt:
        (u_ref, sz_ref, bre_ref, bim_ref, cre_ref, cim_ref, tri_ref, apr_ref, api_ref, anr_ref, ani_ref,
         d_ref, wglu_ref, bglu_ref, o_ref, hre_ref, him_ref, y_scr) = refs
        pre_ref, pim_ref = hre_ref, him_ref

        @pl.when(pl.program_id(0) == 0)
        def _():
            hre_ref[...] = jnp.zeros_like(hre_ref)
            him_ref[...] = jnp.zeros_like(him_ref)
    else:
        (u_ref, sz_ref, bre_ref, bim_ref, cre_ref, cim_ref, tri_ref, apr_ref, api_ref, anr_ref, ani_ref,
         d_ref, wglu_ref, bglu_ref, pre_ref, pim_ref, o_ref, hre_ref, him_ref, y_scr) = refs
    rows = nsub * lt
    tri = tri_ref[...]
    for nt in range(N_SLAB):
        sl = slice(nt * S5_SLAB, (nt + 1) * S5_SLAB)
        u_nt = u_ref[:, nt * S5_USLAB:(nt + 1) * S5_USLAB]
        bu_re = jnp.dot(u_nt, bre_ref[nt], preferred_element_type=F32).reshape(nsub, lt, S5_SLAB)
        bu_im = jnp.dot(u_nt, bim_ref[nt], preferred_element_type=F32).reshape(nsub, lt, S5_SLAB)
        w_re, w_im = _cmul(bu_re, bu_im, anr_ref[0:lt, sl][None], ani_ref[0:lt, sl][None])
        w_re = w_re.reshape(rows, S5_SLAB).astype(BF16)
        w_im = w_im.reshape(rows, S5_SLAB).astype(BF16)
        acc_re = jnp.dot(tri, w_re, preferred_element_type=F32).reshape(nsub, lt, S5_SLAB)
        acc_im = jnp.dot(tri, w_im, preferred_element_type=F32).reshape(nsub, lt, S5_SLAB)
        c_re, c_im = _cmul(apr_ref[1:2, sl], api_ref[1:2, sl], pre_ref[:, sl], pim_ref[:, sl])
        acc_re = acc_re + c_re[:, None, :]
        acc_im = acc_im + c_im[:, None, :]
        h_re, h_im = _cmul(acc_re, acc_im, apr_ref[0:lt, sl][None], api_ref[0:lt, sl][None])
        hre_ref[:, sl] = h_re[:, lt - 1, :]
        him_ref[:, sl] = h_im[:, lt - 1, :]
        h_re = h_re.reshape(rows, S5_SLAB).astype(BF16)
        h_im = h_im.reshape(rows, S5_SLAB).astype(BF16)
        y_scr[:, nt * S5_USLAB:(nt + 1) * S5_USLAB] = (
            lax.dot_general(h_re, cre_ref[nt], _NT, preferred_element_type=F32)
            - lax.dot_general(h_im, cim_ref[nt], _NT, preferred_element_type=F32))
    y = y_scr[...] + d_ref[...] * u_ref[...].astype(F32)
    y = jax.nn.gelu(y)
    gate = jnp.dot(y.astype(BF16), wglu_ref[...], preferred_element_type=F32) + bglu_ref[...]
    y = y * jax.nn.sigmoid(gate)
    o_ref[...] = (y * sz_ref[...].astype(F32)).astype(o_ref.dtype)


def _block_tri(rows, lt):
    r = np.arange(rows)
    same = (r[:, None] // lt) == (r[None, :] // lt)
    return jnp.asarray(np.where(same & (r[:, None] >= r[None, :]), 1.0, 0.0), dtype=BF16)


def _s5_branch(layer, proj, wts, h0_re, h0_im, *, lt, prompt):
    m = proj.shape[0]
    nsub = SEQ_ROWS // lt
    steps = m // SEQ_ROWS
    seqs = nsub if prompt else m // lt
    tri = _block_tri(SEQ_ROWS, lt)
    col = lambda off, w: pl.BlockSpec((SEQ_ROWS, w), lambda c: (c, off // w))
    tab = _layer_spec(layer, (CHUNK, N_STATE))
    in_specs = [col(COL_S5_U, SSM_WIDTH), col(COL_S5_Z, SSM_WIDTH),
                _layer_spec(layer, (N_SLAB, S5_USLAB, S5_SLAB)), _layer_spec(layer, (N_SLAB, S5_USLAB, S5_SLAB)),
                _layer_spec(layer, (N_SLAB, S5_USLAB, S5_SLAB)), _layer_spec(layer, (N_SLAB, S5_USLAB, S5_SLAB)),
                _const_spec((SEQ_ROWS, SEQ_ROWS)), tab, tab, tab, tab,
                _layer_spec(layer, (1, SSM_WIDTH)), _layer_spec(layer, (SSM_WIDTH, SSM_WIDTH)),
                _layer_spec(layer, (1, SSM_WIDTH))]
    args = [proj, proj, wts['s5_bre'], wts['s5_bim'], wts['s5_cre'], wts['s5_cim'], tri,
            wts['s5_apr'], wts['s5_api'], wts['s5_anr'], wts['s5_ani'], wts['s5_d'], wts['w_glu'], wts['b_glu']]
    if prompt:
        state_spec = pl.BlockSpec((nsub, N_STATE), lambda c: (0, 0))
    else:
        state_spec = pl.BlockSpec((nsub, N_STATE), lambda c: (c, 0))
        state_in = pl.BlockSpec((None, nsub, N_STATE), lambda c: (layer, c, 0))
        in_specs += [state_in, state_in]
        args += [h0_re, h0_im]
    state_shape = jax.ShapeDtypeStruct((seqs, N_STATE), F32)
    return pl.pallas_call(
        functools.partial(_s5_body, lt=lt, nsub=nsub, prompt=prompt),
        grid=(steps,),
        in_specs=in_specs,
        out_specs=[pl.BlockSpec((SEQ_ROWS, SSM_WIDTH), lambda c: (c, 0)), state_spec, state_spec],
        out_shape=[jax.ShapeDtypeStruct((m, SSM_WIDTH), BF16), state_shape, state_shape],
        scratch_shapes=[pltpu.VMEM((SEQ_ROWS, SSM_WIDTH), F32)],
        compiler_params=_params(("arbitrary",)),
        name="s5_prompt" if prompt else "s5_sample",
    )(*args)


def _gla_body(*refs, lt, nsub, prompt):
    if prompt:
        (x_ref, q_ref, k_ref, v_ref, gz_ref, wal_ref, bal_ref, wa2_ref, ba_ref, ng_ref, tri_ref,
         o_ref, s_ref, o_scr) = refs
        s0_ref = s_ref

        @pl.when(pl.program_id(0) == 0)
        def _():
            s_ref[...] = jnp.zeros_like(s_ref)
    else:
        (x_ref, q_ref, k_ref, v_ref, gz_ref, wal_ref, bal_ref, wa2_ref, ba_ref, ng_ref, tri_ref,
         s0_ref, _, o_ref, s_ref, o_scr) = refs
    tri = tri_ref[...]
    a_low = lax.dot_general(x_ref[...], wal_ref[...], _NT, preferred_element_type=F32) + bal_ref[...]
    logit = jnp.dot(a_low.astype(BF16), wa2_ref[...], preferred_element_type=F32) + ba_ref[...]
    log_a = jax.nn.log_sigmoid(logit) * (1.0 / GLA_TAU)
    hi = log_a.astype(BF16)
    lo = (log_a - hi.astype(F32)).astype(BF16)
    cum = (jnp.dot(tri, hi, preferred_element_type=F32) + jnp.dot(tri, lo, preferred_element_type=F32))
    q = q_ref[...].astype(F32) * (GLA_DK ** -0.5)
    k = k_ref[...].astype(F32)
    qd = (q * jnp.exp(cum)).astype(BF16)
    kd = (k * jnp.exp(-cum)).astype(BF16)
    cum3 = cum.reshape(nsub, lt, GLA_KEY_WIDTH)
    last = cum3[:, lt - 1:lt, :]
    kk = (k.reshape(nsub, lt, GLA_KEY_WIDTH) * jnp.exp(last - cum3)).astype(BF16)
    e_last = jnp.exp(last)
    ri = lax.broadcasted_iota(jnp.int32, (lt, lt), 0)
    ci = lax.broadcasted_iota(jnp.int32, (lt, lt), 1)
    causal = ri >= ci
    combos = [(s, h) for s in range(nsub) for h in range(GLA_HEADS)]
    rs = lambda s: slice(s * lt, (s + 1) * lt)
    ks = lambda h: slice(h * GLA_DK, (h + 1) * GLA_DK)
    vs = lambda h: slice(h * GLA_DV, (h + 1) * GLA_DV)
    atts = []
    for s, h in combos:
        att = lax.dot_general(qd[rs(s), ks(h)], kd[rs(s), ks(h)], _NT, preferred_element_type=F32)
        atts.append(jnp.where(causal, att, 0.0).astype(BF16))
    for (s, h), att in zip(combos, atts):
        o_scr[rs(s), vs(h)] = (jnp.dot(att, v_ref[rs(s), vs(h)], preferred_element_type=F32)
                               + jnp.dot(qd[rs(s), ks(h)], s0_ref[s, h].astype(BF16),
                                         preferred_element_type=F32))
    for s, h in combos:
        kv = lax.dot_general(kk[s, :, ks(h)], v_ref[rs(s), vs(h)], (((0,), (0,)), ((), ())),
                             preferred_element_type=F32)
        dec = jnp.transpose(jnp.broadcast_to(e_last[s, :, ks(h)], (GLA_DK, GLA_DK)))
        s_ref[s, h] = jnp.concatenate([dec, dec], axis=1) * s0_ref[s, h] + kv
    for h in range(GLA_HEADS):
        o = o_scr[:, vs(h)]
        mu = jnp.mean(o, axis=-1, keepdims=True)
        var = jnp.mean(jnp.square(o - mu), axis=-1, keepdims=True)
        o = (o - mu) * lax.rsqrt(var + LN_EPS) * ng_ref[:, vs(h)]
        o_ref[:, vs(h)] = (o * gz_ref[:, vs(h)].astype(F32)).astype(o_ref.dtype)


def _gla_branch(layer, xb, proj, wts, s0, s_stack, *, lt, prompt):
    m = proj.shape[0]
    rows = SEQ_ROWS if prompt else GLA_SAMPLE_ROWS
    nsub = rows // lt
    steps = m // rows
    tri = _block_tri(rows, lt)
    col = lambda off, w: pl.BlockSpec((rows, w), lambda c: (c, off // w))
    in_specs = [pl.BlockSpec((rows, D_MODEL), lambda c: (c, 0)),
                col(COL_Q, GLA_KEY_WIDTH), col(COL_K, GLA_KEY_WIDTH), col(COL_V, GLA_VAL_WIDTH),
                col(COL_GLA_Z, GLA_VAL_WIDTH),
                _layer_spec(layer, (ALOW_PAD, D_MODEL)), _layer_spec(layer, (1, ALOW_PAD)),
                _layer_spec(layer, (ALOW_PAD, GLA_KEY_WIDTH)), _layer_spec(layer, (1, GLA_KEY_WIDTH)),
                _layer_spec(layer, (1, GLA_VAL_WIDTH)), _const_spec((rows, rows))]
    args = [xb, proj, proj, proj, proj, wts['w_alow'], wts['b_alow'], wts['gla_w_a2'], wts['gla_b_a'],
            wts['gla_norm_g'], tri]
    state_dims = (GLA_HEADS, GLA_DK, GLA_DV)
    aliases = {}
    if prompt:
        state_spec = pl.BlockSpec((nsub,) + state_dims, lambda c: (0, 0, 0, 0))
        state_shape = jax.ShapeDtypeStruct((nsub,) + state_dims, F32)
    else:
        state_spec = pl.BlockSpec((None, nsub) + state_dims, lambda c: (layer, c, 0, 0, 0))
        state_shape = jax.ShapeDtypeStruct(s0.shape, F32)
        in_specs += [state_spec, pl.BlockSpec(memory_space=pl.ANY)]
        args += [s0, s_stack]
        aliases = {len(args) - 1: 1}
    return pl.pallas_call(
        functools.partial(_gla_body, lt=lt, nsub=nsub, prompt=prompt),
        grid=(steps,),
        in_specs=in_specs,
        out_specs=[pl.BlockSpec((rows, GLA_VAL_WIDTH), lambda c: (c, 0)), state_spec],
        out_shape=[jax.ShapeDtypeStruct((m, GLA_VAL_WIDTH), BF16), state_shape],
        scratch_shapes=[pltpu.VMEM((rows, GLA_VAL_WIDTH), F32)],
        input_output_aliases=aliases,
        compiler_params=_params(("arbitrary",)),
        name="gla_prompt" if prompt else "gla_sample",
    )(*args)


def _conv_epilogue(y, b_ref, lng_ref, lnb_ref, cz_ref, o_ref):
    y = y + b_ref[...]
    mu = jnp.mean(y, axis=-1, keepdims=True)
    var = jnp.mean(jnp.square(y - mu), axis=-1, keepdims=True)
    y = (y - mu) * lax.rsqrt(var + LN_EPS) * lng_ref[...] + lnb_ref[...]
    o_ref[...] = (jax.nn.silu(y) * cz_ref[...].astype(F32)).astype(o_ref.dtype)


def _conv_sample_body(ca_ref, scb_ref, cz_ref, w_ref, b_ref, lng_ref, lnb_ref, cache_ref, _,
                      o_ref, cache_out_ref, x_scr, y_scr, *, lt, nsub):
    pad = CONV_HIST - (CONV_K - 1)
    taps_by_shift = [[j for j in range(CONV_K) if (j + pad) % SUBLANES == sh] for sh in range(SUBLANES)]

    def per_seq(s, carry):
        r0 = pl.multiple_of(s * lt, SUBLANES)
        x_scr[s, CONV_HIST:CONV_HIST + lt, :] = (ca_ref[pl.ds(r0, lt), :].astype(F32)
                                                 * scb_ref[pl.ds(r0, lt), :].astype(F32))
        x_scr[s, pad:CONV_HIST, :] = cache_ref[s]
        acc = None
        for sh, taps in enumerate(taps_by_shift):
            span = max(taps) + pad - sh + lt
            xs = x_scr[s, pl.ds(sh, span), :]
            part = None
            for j in taps:
                base = j + pad - sh
                term = w_ref[j:j + 1, :] * xs[base:base + lt, :]
                part = term if part is None else part + term
            acc = part if acc is None else acc + part
        y_scr[pl.ds(r0, lt), :] = acc
        cache_out_ref[s] = x_scr[s, lt + pad:lt + CONV_HIST, :]
        return carry

    lax.fori_loop(0, nsub, per_seq, 0)
    _conv_epilogue(y_scr[...], b_ref, lng_ref, lnb_ref, cz_ref, o_ref)


CONV_STRIDE = 4
CONV_GROUP = SUBLANES * CONV_STRIDE
N_LANE_TILE = CONV_WIDTH // LANES


def _conv_prompt_body(ca_ref, scb_ref, cz_ref, w_ref, b_ref, lng_ref, lnb_ref, o_ref, cache_out_ref,
                      x_scr, y_scr, wb_scr, *, nsub):
    lt = CHUNK
    pitch = CONV_HIST + lt
    pad = CONV_HIST - (CONV_K - 1)

    @pl.when(pl.program_id(0) == 0)
    def _():
        for s in range(nsub):
            x_scr[:, s * pitch:s * pitch + CONV_HIST, :] = jnp.zeros((N_LANE_TILE, CONV_HIST, LANES), F32)
        for j in range(CONV_K):
            wb_scr[j] = jnp.broadcast_to(w_ref[j:j + 1, :], (SUBLANES, CONV_WIDTH))

    def per_seq(s, carry):
        r0 = pl.multiple_of(s * lt, SUBLANES)
        x0 = pl.multiple_of(s * pitch, SUBLANES)
        g = ca_ref[pl.ds(r0, lt), :].astype(F32) * scb_ref[pl.ds(r0, lt), :].astype(F32)
        for l in range(N_LANE_TILE):
            x_scr[l, pl.ds(x0 + CONV_HIST, lt), :] = g[:, l * LANES:(l + 1) * LANES]
        for l in range(N_LANE_TILE):
            ls = slice(l * LANES, (l + 1) * LANES)
            outs = [grp + k for grp in range(0, lt, CONV_GROUP) for k in range(CONV_STRIDE)]
            accs = [None] * len(outs)
            for j in range(CONV_K):
                wj = wb_scr[j, :, ls]
                for i, t0 in enumerate(outs):
                    term = wj * x_scr[l, pl.ds(x0 + t0 + j + pad, SUBLANES, stride=CONV_STRIDE), :]
                    accs[i] = term if accs[i] is None else accs[i] + term
            for i, t0 in enumerate(outs):
                y_scr[l, pl.ds(r0 + t0, SUBLANES, stride=CONV_STRIDE), :] = accs[i]
            cache_out_ref[s, :, ls] = x_scr[l, pl.ds(x0 + lt + pad, CONV_K - 1), :]
            x_scr[l, pl.ds(x0, CONV_HIST), :] = x_scr[l, pl.ds(x0 + lt, CONV_HIST), :]
        return carry

    lax.fori_loop(0, nsub, per_seq, 0)
    y = jnp.concatenate([y_scr[l] for l in range(N_LANE_TILE)], axis=1)
    _conv_epilogue(y, b_ref, lng_ref, lnb_ref, cz_ref, o_ref)


def _conv_branch(layer, proj, wts, cache, cache_stack, *, lt, prompt):
    m = proj.shape[0]
    nsub = SEQ_ROWS // lt
    steps = m // SEQ_ROWS
    col = lambda off: pl.BlockSpec((SEQ_ROWS, CONV_WIDTH), lambda c: (c, off // CONV_WIDTH))
    in_specs = [col(COL_CA), col(COL_CB), col(COL_CZ),
                _layer_spec(layer, (CONV_K, CONV_WIDTH)), _layer_spec(layer, (1, CONV_WIDTH)),
                _layer_spec(layer, (1, CONV_WIDTH)), _layer_spec(layer, (1, CONV_WIDTH))]
    args = [proj, proj, proj, wts['conv_w'], wts['conv_b'], wts['conv_ln_g'], wts['conv_ln_b']]
    aliases = {}
    if prompt:
        cache_spec = pl.BlockSpec((nsub, CONV_K - 1, CONV_WIDTH), lambda c: (0, 0, 0))
        cache_shape = jax.ShapeDtypeStruct((nsub, CONV_K - 1, CONV_WIDTH), F32)
    else:
        cache_spec = pl.BlockSpec((None, nsub, CONV_K - 1, CONV_WIDTH), lambda c: (layer, c, 0, 0))
        cache_shape = jax.ShapeDtypeStruct(cache.shape, F32)
        in_specs += [cache_spec, pl.BlockSpec(memory_space=pl.ANY)]
        args += [cache, cache_stack]
        aliases = {len(args) - 1: 1}
    if prompt:
        body = functools.partial(_conv_prompt_body, nsub=nsub)
        scratch = [pltpu.VMEM((N_LANE_TILE, nsub * (CONV_HIST + lt), LANES), F32),
                   pltpu.VMEM((N_LANE_TILE, SEQ_ROWS, LANES), F32),
                   pltpu.VMEM((CONV_K, SUBLANES, CONV_WIDTH), F32)]
    else:
        body = functools.partial(_conv_sample_body, lt=lt, nsub=nsub)
        scratch = [pltpu.VMEM((nsub, CONV_HIST + lt, CONV_WIDTH), F32), pltpu.VMEM((SEQ_ROWS, CONV_WIDTH), F32)]
    return pl.pallas_call(
        body,
        grid=(steps,),
        in_specs=in_specs,
        out_specs=[pl.BlockSpec((SEQ_ROWS, CONV_WIDTH), lambda c: (c, 0)), cache_spec],
        out_shape=[jax.ShapeDtypeStruct((m, CONV_WIDTH), BF16), cache_shape],
        scratch_shapes=scratch,
        input_output_aliases=aliases,
        compiler_params=_params(("arbitrary",)),
        name="conv_prompt" if prompt else "conv_sample",
    )(*args)


def _merge_body(a1_ref, a2_ref, a3_ref, g1_ref, g2_ref, g3_ref, p1_ref, p2_ref, p3_ref, o_ref):
    m = g1_ref[...].astype(F32) * jnp.dot(a1_ref[...], p1_ref[...], preferred_element_type=F32)
    m = m + g2_ref[...].astype(F32) * jnp.dot(a2_ref[...], p2_ref[...], preferred_element_type=F32)
    m = m + g3_ref[...].astype(F32) * jnp.dot(a3_ref[...], p3_ref[...], preferred_element_type=F32)
    o_ref[...] = m.astype(o_ref.dtype)


def _merge(layer, a_s5, a_gla, a_conv, proj, wts):
    m = a_s5.shape[0]
    rows = min(MERGE_ROWS, m)
    act = pl.BlockSpec((rows, SSM_WIDTH), lambda i: (i, 0))
    gate = lambda n: pl.BlockSpec((rows, D_MODEL), lambda i: (i, COL_GATES // D_MODEL + n))
    wspec = _layer_spec(layer, (SSM_WIDTH, D_MODEL))
    return pl.pallas_call(
        _merge_body,
        grid=(m // rows,),
        in_specs=[act, act, act, gate(0), gate(1), gate(2), wspec, wspec, wspec],
        out_specs=pl.BlockSpec((rows, D_MODEL), lambda i: (i, 0)),
        out_shape=jax.ShapeDtypeStruct((m, D_MODEL), BF16),
        compiler_params=_params(("arbitrary",)),
        name="merge",
    )(a_s5, a_gla, a_conv, proj, proj, proj, wts['p_s5'], wts['p_gla'], wts['p_conv'])


def _post_body(m_ref, x_ref, p_ref, wo_ref, wpg_ref, wpe_ref, g_ref, b_ref, xo_ref, xb_ref):
    def piece(ref, i):
        if len(ref.shape) == 3:
            return ref[:, i * CHUNK:(i + 1) * CHUNK, :].reshape(SEQ_ROWS, ref.shape[-1])
        return ref[i * SEQ_ROWS:(i + 1) * SEQ_ROWS, :]

    for i in range(m_ref.shape[0] // SEQ_ROWS):
        h = DN_ALPHA * piece(x_ref, i) + jnp.dot(piece(m_ref, i), wo_ref[...], preferred_element_type=F32)
        gate = jax.nn.sigmoid(jnp.dot(h.astype(BF16), wpg_ref[...], preferred_element_type=F32))
        h = h + gate * jnp.dot(piece(p_ref, i).astype(BF16), wpe_ref[...], preferred_element_type=F32)
        mu = jnp.mean(h, axis=-1, keepdims=True)
        var = jnp.mean(jnp.square(h - mu), axis=-1, keepdims=True)
        xn = (h - mu) * lax.rsqrt(var + LN_EPS) * g_ref[...] + b_ref[...]
        if len(xo_ref.shape) == 3:
            xo_ref[:, i * CHUNK:(i + 1) * CHUNK, :] = xn.reshape(xo_ref.shape[0], CHUNK, D_MODEL)
        else:
            xo_ref[i * SEQ_ROWS:(i + 1) * SEQ_ROWS, :] = xn
        xb_ref[i * SEQ_ROWS:(i + 1) * SEQ_ROWS, :] = xn.astype(BF16)


def _post(layer, merged, x, p, wts, *, prompt, batch_major_out=False):
    m = merged.shape[0]
    rows = min(MERGE_ROWS, m)
    row = lambda w: pl.BlockSpec((rows, w), lambda i: (i, 0))
    if prompt:
        bp = p.shape[1]
        p_spec = pl.BlockSpec((None, bp, rows // bp, PLE_DIM), lambda i: (layer, 0, i, 0))
        bm_spec = pl.BlockSpec((bp, rows // bp, D_MODEL), lambda i: (0, i, 0))
    else:
        p_spec = pl.BlockSpec((None, rows, PLE_DIM), lambda i: (layer, i, 0))
    x_spec = bm_spec if x.ndim == 3 else row(D_MODEL)
    if batch_major_out:
        xo_spec, xo_shape = bm_spec, jax.ShapeDtypeStruct((bp, m // bp, D_MODEL), F32)
    else:
        xo_spec, xo_shape = row(D_MODEL), jax.ShapeDtypeStruct((m, D_MODEL), F32)
    return pl.pallas_call(
        _post_body,
        grid=(m // rows,),
        in_specs=[row(D_MODEL), x_spec, p_spec,
                  _layer_spec(layer, (D_MODEL, D_MODEL)), _layer_spec(layer, (D_MODEL, D_MODEL)),
                  _layer_spec(layer, (PLE_DIM, D_MODEL)),
                  _layer_spec(layer, (1, D_MODEL)), _layer_spec(layer, (1, D_MODEL))],
        out_specs=[xo_spec, row(D_MODEL)],
        out_shape=[xo_shape, jax.ShapeDtypeStruct((m, D_MODEL), BF16)],
        compiler_params=_params(("arbitrary",)),
        name="post",
    )(merged, x, p, wts['w_o'], wts['w_pg'], wts['w_pe'], wts['ln_g'], wts['ln_b'])


def _layer(layer, x, xb, p, wts, states, stacks, *, lt, prompt, batch_major_out=False):
    s5_re, s5_im, gla_s, conv_buf = states
    gla_stack, conv_stack = stacks
    proj = _inproj(layer, xb, wts)
    a_s5, h_re, h_im = _s5_branch(layer, proj, wts, s5_re, s5_im, lt=lt, prompt=prompt)
    a_gla, gla_new = _gla_branch(layer, xb, proj, wts, gla_s, gla_stack, lt=lt, prompt=prompt)
    a_conv, conv_new = _conv_branch(layer, proj, wts, conv_buf, conv_stack, lt=lt, prompt=prompt)
    merged = _merge(layer, a_s5, a_gla, a_conv, proj, wts)
    x_new, xb_new = _post(layer, merged, x, p, wts, prompt=prompt, batch_major_out=batch_major_out)
    return x_new, xb_new, h_re, h_im, gla_new, conv_new


def _to_chunk_major(a, bsz, seq):
    w = a.shape[-1]
    a = a.reshape(bsz, seq // CHUNK, CHUNK, w)
    return jnp.transpose(a, (1, 0, 2, 3)).reshape(bsz * seq, w)


def kernel(x_prompt, x_sample, p_prompt, p_sample, state_s5_re, state_s5_im, state_gla, cache_conv, w_in, b_in, s5_a_re, s5_a_im, s5_log_dt, s5_b_re, s5_b_im, s5_c_re, s5_c_im, s5_d, w_glu, b_glu, gla_w_a2, gla_b_a, gla_norm_g, conv_w, conv_b, conv_ln_g, conv_ln_b, p_s5, p_gla, p_conv, w_o, w_pg, w_pe, ln_g, ln_b):
    bp, seq, _ = x_prompt.shape
    bs, dseq, _ = x_sample.shape
    depth = w_in.shape[0]
    assert seq % CHUNK == 0 and bp * CHUNK == SEQ_ROWS and (bs * dseq) % SEQ_ROWS == 0 and dseq == SUBLANES

    apr, api, anr, ani, bbr, bbi = _s5_prep(s5_a_re, s5_a_im, s5_log_dt, s5_b_re, s5_b_im)
    s5_bre, s5_bim = _s5_block_diag(bbr), _s5_block_diag(bbi)
    s5_cre, s5_cim = _s5_block_diag(_s5_out_map(s5_c_re)), _s5_block_diag(_s5_out_map(s5_c_im))
    row = lambda a: a[:, None, :]
    alow = slice(*_IN_SRC['alow'])
    pad_rank = (0, ALOW_PAD - GLA_GATE_RANK)
    w_in_t = jnp.transpose(w_in, (0, 2, 1))
    wts = dict(
        w_in=w_in_t, in_table=_in_block_table(), b_in=_reorder_bias(b_in),
        w_alow=jnp.pad(w_in_t[:, alow, :], ((0, 0), pad_rank, (0, 0))).astype(BF16),
        b_alow=row(jnp.pad(b_in[:, alow], ((0, 0), pad_rank))),
        s5_bre=s5_bre, s5_bim=s5_bim, s5_cre=s5_cre, s5_cim=s5_cim,
        s5_apr=apr, s5_api=api, s5_anr=anr, s5_ani=ani,
        s5_d=row(s5_d), w_glu=w_glu.astype(BF16), b_glu=row(b_glu),
        gla_w_a2=jnp.pad(gla_w_a2, ((0, 0), pad_rank, (0, 0))).astype(BF16),
        gla_b_a=row(gla_b_a), gla_norm_g=row(gla_norm_g),
        conv_w=conv_w, conv_b=row(conv_b), conv_ln_g=row(conv_ln_g), conv_ln_b=row(conv_ln_b),
        p_s5=p_s5.astype(BF16), p_gla=p_gla.astype(BF16), p_conv=p_conv.astype(BF16),
        w_o=w_o.astype(BF16), w_pg=w_pg.astype(BF16), w_pe=w_pe.astype(BF16),
        ln_g=row(ln_g), ln_b=row(ln_b))

    hp, hpb = x_prompt, _to_chunk_major(x_prompt.astype(BF16), bp, seq)
    hs = x_sample.reshape(bs * dseq, D_MODEL)
    hsb = hs.astype(BF16)
    ps = p_sample.reshape(depth, bs * dseq, PLE_DIM)
    sample_states = (state_s5_re.reshape(depth, bs, N_STATE), state_s5_im.reshape(depth, bs, N_STATE),
                     state_gla, cache_conv)
    pr_re, pr_im, pr_gla, pr_conv, sm_re, sm_im = [], [], [], [], [], []
    gla_stack, conv_stack = jnp.zeros(state_gla.shape, F32), jnp.zeros(cache_conv.shape, F32)
    for i in range(depth):
        hp, hpb, a, b, c, d = _layer(i, hp, hpb, p_prompt, wts, (None,) * 4, (None, None), lt=CHUNK, prompt=True,
                                     batch_major_out=i == depth - 1)
        pr_re.append(a); pr_im.append(b); pr_gla.append(c); pr_conv.append(d)
        hs, hsb, a, b, gla_stack, conv_stack = _layer(i, hs, hsb, ps, wts, sample_states, (gla_stack, conv_stack),
                                                      lt=dseq, prompt=False)
        sm_re.append(a); sm_im.append(b)
    s5_shape = lambda a, n: jnp.stack(a).reshape(depth, n, SSM_GROUPS, SSM_STATE)
    return (hp, hs.reshape(bs, dseq, D_MODEL),
            s5_shape(pr_re, bp), s5_shape(pr_im, bp), jnp.stack(pr_gla), jnp.stack(pr_conv),
            s5_shape(sm_re, bs), s5_shape(sm_im, bs), gla_stack, conv_stack)
```

```python
import functools

import numpy as np
import jax
import jax.numpy as jnp
from jax import lax
from jax.experimental import pallas as pl
from jax.experimental.pallas import tpu as pltpu

F32 = jnp.float32
BF16 = jnp.bfloat16

D_MODEL = 2048
PLE_DIM = 256
SSM_WIDTH = 1024
SSM_GROUP = 16
SSM_GROUPS = 64
SSM_STATE = 64
N_STATE = SSM_GROUPS * SSM_STATE
GLA_HEADS = 4
GLA_DK = 128
GLA_DV = 256
GLA_KEY_WIDTH = 512
GLA_VAL_WIDTH = 1024
GLA_GATE_RANK = 16
GLA_TAU = 16.0
CONV_WIDTH = 1024
CONV_K = 31
N_BRANCH = 3
DN_ALPHA = 8.0 ** 0.25
LN_EPS = 1e-5

SUBLANES = 8
LANES = 128
CONV_HIST = 32
CHUNK = 64
SEQ_ROWS = 256
GLA_SAMPLE_ROWS = 64
S5_SLAB = 1024
S5_USLAB = 256
N_SLAB = N_STATE // S5_SLAB
MERGE_ROWS = 512

VMEM_LIMIT = 56 * 1024 * 1024

PROJ_BN = 512
PROJ_BM = 4096
PROJ_SUB = 512
ALOW_PAD = LANES

_IN_SIZES = [SSM_WIDTH, SSM_WIDTH, GLA_KEY_WIDTH, GLA_KEY_WIDTH, GLA_VAL_WIDTH, GLA_GATE_RANK, GLA_VAL_WIDTH,
             CONV_WIDTH, CONV_WIDTH, CONV_WIDTH, N_BRANCH * D_MODEL]
_IN_NAMES = ['s5_u', 's5_z', 'q', 'k', 'v', 'alow', 'gla_z', 'ca', 'cb', 'cz', 'gates']
_IN_OFFS = np.concatenate([[0], np.cumsum(_IN_SIZES)])
_IN_SRC = {n: (int(_IN_OFFS[i]), int(_IN_OFFS[i + 1])) for i, n in enumerate(_IN_NAMES)}
_IN_ORDER = ['gates', 's5_u', 'v', 'ca', 's5_z', 'gla_z', 'cz', 'cb', 'q', 'k']
_COL = dict(zip(_IN_ORDER, np.concatenate([[0], np.cumsum([_IN_SRC[n][1] - _IN_SRC[n][0]
                                                             for n in _IN_ORDER])]).tolist()))
COL_GATES, COL_S5_U, COL_V, COL_CA = _COL['gates'], _COL['s5_u'], _COL['v'], _COL['ca']
COL_S5_Z, COL_GLA_Z, COL_CZ, COL_CB = _COL['s5_z'], _COL['gla_z'], _COL['cz'], _COL['cb']
COL_Q, COL_K = _COL['q'], _COL['k']
PROJ_COLS = COL_K + GLA_KEY_WIDTH


def _layer_spec(layer, shape):
    zeros = (0,) * len(shape)
    return pl.BlockSpec((None,) + tuple(shape), lambda *_: (layer,) + zeros, pipeline_mode=pl.Buffered(1))


def _const_spec(shape):
    zeros = (0,) * len(shape)
    return pl.BlockSpec(tuple(shape), lambda *_: zeros, pipeline_mode=pl.Buffered(1))


def _params(sem):
    return pltpu.CompilerParams(dimension_semantics=sem, vmem_limit_bytes=VMEM_LIMIT)


def _in_block_table():
    src = []
    for name in _IN_ORDER:
        lo, hi = _IN_SRC[name]
        src += list(range(lo, hi, PROJ_BN))
    assert all(lo % GLA_GATE_RANK == 0 for lo in src)
    return jnp.asarray(np.asarray(src, np.int32) // GLA_GATE_RANK)


def _reorder_bias(b_in):
    parts = [b_in[:, _IN_SRC[n][0]:_IN_SRC[n][1]] for n in _IN_ORDER]
    return jnp.concatenate(parts, axis=-1)[:, None, :]


_NT = (((1,), (1,)), ((), ()))


def _inproj_body(tbl_ref, x_ref, w_ref, b_ref, o_ref, w_scr):
    j = pl.program_id(1)
    is_sig = (j < COL_S5_U // PROJ_BN) | ((j >= COL_CB // PROJ_BN) & (j < COL_Q // PROJ_BN))
    is_silu = (j >= COL_S5_Z // PROJ_BN) & (j < COL_CB // PROJ_BN)
    w_scr[...] = w_ref[0].astype(BF16)

    def run(act):
        for r in range(0, x_ref.shape[0], PROJ_SUB):
            acc = lax.dot_general(x_ref[r:r + PROJ_SUB, :], w_scr[...], _NT, preferred_element_type=F32)
            o_ref[r:r + PROJ_SUB, :] = act(acc + b_ref[...]).astype(o_ref.dtype)

    pl.when(is_sig)(lambda: run(jax.nn.sigmoid))
    pl.when(is_silu)(lambda: run(jax.nn.silu))
    pl.when(jnp.logical_not(is_sig | is_silu))(lambda: run(lambda a: a))


def _inproj(layer, x, wts):
    m = x.shape[0]
    bm = min(m, PROJ_BM)
    grid_spec = pltpu.PrefetchScalarGridSpec(
        num_scalar_prefetch=1,
        grid=(m // bm, PROJ_COLS // PROJ_BN),
        in_specs=[
            pl.BlockSpec((bm, D_MODEL), lambda i, j, tbl: (i, 0), pipeline_mode=pl.Buffered(1)),
            pl.BlockSpec((pl.Element(1), pl.Element(PROJ_BN), pl.Element(D_MODEL)),
                         lambda i, j, tbl: (layer, tbl[j] * GLA_GATE_RANK, 0)),
            pl.BlockSpec((None, 1, PROJ_BN), lambda i, j, tbl: (layer, 0, j))],
        out_specs=pl.BlockSpec((bm, PROJ_BN), lambda i, j, tbl: (i, j)),
        scratch_shapes=[pltpu.VMEM((PROJ_BN, D_MODEL), BF16)])
    return pl.pallas_call(
        _inproj_body,
        grid_spec=grid_spec,
        out_shape=jax.ShapeDtypeStruct((m, PROJ_COLS), BF16),
        compiler_params=_params(("arbitrary", "arbitrary")),
        name="inproj",
    )(wts['in_table'], x, wts['w_in'], wts['b_in'])


def _s5_prep_body(ar_ref, ai_ref, ldt_ref, bre_ref, bim_ref, apr_ref, api_ref, anr_ref, ani_ref, bbr_ref, bbi_ref):
    ar = ar_ref[0]
    ai = ai_ref[0]
    dt = jnp.exp(ldt_ref[0])
    xr = ar * dt
    xi = ai * dt
    t = lax.broadcasted_iota(jnp.int32, (CHUNK, N_STATE), 0).astype(F32)
    ang = t * xi
    cs = jnp.cos(ang)
    sn = jnp.sin(ang)
    mag = jnp.exp(t * xr)
    imag = jnp.exp(-(t * xr))
    apr_ref[0] = mag * cs
    api_ref[0] = mag * sn
    anr_ref[0] = imag * cs
    ani_ref[0] = -(imag * sn)
    e1 = jnp.exp(xr)
    nr = e1 * jnp.cos(xi) - 1.0
    ni = e1 * jnp.sin(xi)
    den = ar * ar + ai * ai
    cr = (nr * ar + ni * ai) / den
    ci = (ni * ar - nr * ai) / den
    bre = bre_ref[0]
    bim = bim_ref[0]
    bbr_ref[0] = cr * bre - ci * bim
    bbi_ref[0] = cr * bim + ci * bre


def _s5_prep(a_re, a_im, log_dt, b_re, b_im):
    depth = a_re.shape[0]
    ar = a_re.reshape(depth, 1, N_STATE)
    ai = a_im.reshape(depth, 1, N_STATE)
    ldt = jnp.broadcast_to(log_dt[:, :, None], (depth, SSM_GROUPS, SSM_STATE)).reshape(depth, 1, N_STATE)
    bre = jnp.transpose(b_re, (0, 3, 1, 2)).reshape(depth, SSM_GROUP, N_STATE)
    bim = jnp.transpose(b_im, (0, 3, 1, 2)).reshape(depth, SSM_GROUP, N_STATE)
    row = pl.BlockSpec((1, 1, N_STATE), lambda i: (i, 0, 0))
    grp = pl.BlockSpec((1, SSM_GROUP, N_STATE), lambda i: (i, 0, 0))
    tab = pl.BlockSpec((1, CHUNK, N_STATE), lambda i: (i, 0, 0))
    tab_shape = jax.ShapeDtypeStruct((depth, CHUNK, N_STATE), F32)
    grp_shape = jax.ShapeDtypeStruct((depth, SSM_GROUP, N_STATE), F32)
    return pl.pallas_call(
        _s5_prep_body,
        grid=(depth,),
        in_specs=[row, row, row, grp, grp],
        out_specs=[tab, tab, tab, tab, grp, grp],
        out_shape=[tab_shape, tab_shape, tab_shape, tab_shape, grp_shape, grp_shape],
        compiler_params=_params(("arbitrary",)),
        name="s5_prep",
    )(ar, ai, ldt, bre, bim)


def _s5_block_diag(a):
    depth = a.shape[0]
    a4 = jnp.transpose(a.reshape(depth, SSM_GROUP, N_SLAB, S5_SLAB), (0, 2, 1, 3))
    tiled = jnp.tile(a4, (1, 1, S5_USLAB // SSM_GROUP, 1))
    r = np.arange(S5_USLAB)[:, None] // SSM_GROUP
    c = np.arange(S5_SLAB)[None, :] // SSM_STATE
    return jnp.where(jnp.asarray(r == c), tiled, 0.0).astype(BF16)


def _s5_out_map(c):
    return jnp.transpose(c, (0, 2, 1, 3)).reshape(c.shape[0], SSM_GROUP, N_STATE)


def _cmul(ar, ai, br, bi):
    return ar * br - ai * bi, ar * bi + ai * br


def _s5_body(*refs, lt, nsub, prompt):
    if prompt:
        (u_ref, sz_ref, bre_ref, bim_ref, cre_ref, cim_ref, tri_ref, apr_ref, api_ref, anr_ref, ani_ref,
         d_ref, wglu_ref, bglu_ref, o_ref, hre_ref, him_ref, y_scr) = refs
        pre_ref, pim_ref = hre_ref, him_ref

        @pl.when(pl.program_id(0) == 0)
        def _():
            hre_ref[...] = jnp.zeros_like(hre_ref)
            him_ref[...] = jnp.zeros_like(him_ref)
    else:
        (u_ref, sz_ref, bre_ref, bim_ref, cre_ref, cim_ref, tri_ref, apr_ref, api_ref, anr_ref, ani_ref,
         d_ref, wglu_ref, bglu_ref, pre_ref, pim_ref, o_ref, hre_ref, him_ref, y_scr) = refs
    rows = nsub * lt
    tri = tri_ref[...]
    for nt in range(N_SLAB):
        sl = slice(nt * S5_SLAB, (nt + 1) * S5_SLAB)
        u_nt = u_ref[:, nt * S5_USLAB:(nt + 1) * S5_USLAB]
        bu_re = jnp.dot(u_nt, bre_ref[nt], preferred_element_type=F32).reshape(nsub, lt, S5_SLAB)
        bu_im = jnp.dot(u_nt, bim_ref[nt], preferred_element_type=F32).reshape(nsub, lt, S5_SLAB)
        w_re, w_im = _cmul(bu_re, bu_im, anr_ref[0:lt, sl][None], ani_ref[0:lt, sl][None])
        w_re = w_re.reshape(rows, S5_SLAB).astype(BF16)
        w_im = w_im.reshape(rows, S5_SLAB).astype(BF16)
        acc_re = jnp.dot(tri, w_re, preferred_element_type=F32).reshape(nsub, lt, S5_SLAB)
        acc_im = jnp.dot(tri, w_im, preferred_element_type=F32).reshape(nsub, lt, S5_SLAB)
        c_re, c_im = _cmul(apr_ref[1:2, sl], api_ref[1:2, sl], pre_ref[:, sl], pim_ref[:, sl])
        acc_re = acc_re + c_re[:, None, :]
        acc_im = acc_im + c_im[:, None, :]
        h_re, h_im = _cmul(acc_re, acc_im, apr_ref[0:lt, sl][None], api_ref[0:lt, sl][None])
        hre_ref[:, sl] = h_re[:, lt - 1, :]
        him_ref[:, sl] = h_im[:, lt - 1, :]
        h_re = h_re.reshape(rows, S5_SLAB).astype(BF16)
        h_im = h_im.reshape(rows, S5_SLAB).astype(BF16)
        y_scr[:, nt * S5_USLAB:(nt + 1) * S5_USLAB] = (
            lax.dot_general(h_re, cre_ref[nt], _NT, preferred_element_type=F32)
            - lax.dot_general(h_im, cim_ref[nt], _NT, preferred_element_type=F32))
    y = y_scr[...] + d_ref[...] * u_ref[...].astype(F32)
    y = jax.nn.gelu(y)
    gate = jnp.dot(y.astype(BF16), wglu_ref[...], preferred_element_type=F32) + bglu_ref[...]
    y = y * jax.nn.sigmoid(gate)
    o_ref[...] = (y * sz_ref[...].astype(F32)).astype(o_ref.dtype)


def _block_tri(rows, lt):
    r = np.arange(rows)
    same = (r[:, None] // lt) == (r[None, :] // lt)
    return jnp.asarray(np.where(same & (r[:, None] >= r[None, :]), 1.0, 0.0), dtype=BF16)


def _s5_branch(layer, proj, wts, h0_re, h0_im, *, lt, prompt):
    m = proj.shape[0]
    nsub = SEQ_ROWS // lt
    steps = m // SEQ_ROWS
    seqs = nsub if prompt else m // lt
    tri = _block_tri(SEQ_ROWS, lt)
    col = lambda off, w: pl.BlockSpec((SEQ_ROWS, w), lambda c: (c, off // w))
    tab = _layer_spec(layer, (CHUNK, N_STATE))
    in_specs = [col(COL_S5_U, SSM_WIDTH), col(COL_S5_Z, SSM_WIDTH),
                _layer_spec(layer, (N_SLAB, S5_USLAB, S5_SLAB)), _layer_spec(layer, (N_SLAB, S5_USLAB, S5_SLAB)),
                _layer_spec(layer, (N_SLAB, S5_USLAB, S5_SLAB)), _layer_spec(layer, (N_SLAB, S5_USLAB, S5_SLAB)),
                _const_spec((SEQ_ROWS, SEQ_ROWS)), tab, tab, tab, tab,
                _layer_spec(layer, (1, SSM_WIDTH)), _layer_spec(layer, (SSM_WIDTH, SSM_WIDTH)),
                _layer_spec(layer, (1, SSM_WIDTH))]
    args = [proj, proj, wts['s5_bre'], wts['s5_bim'], wts['s5_cre'], wts['s5_cim'], tri,
            wts['s5_apr'], wts['s5_api'], wts['s5_anr'], wts['s5_ani'], wts['s5_d'], wts['w_glu'], wts['b_glu']]
    if prompt:
        state_spec = pl.BlockSpec((nsub, N_STATE), lambda c: (0, 0))
    else:
        state_spec = pl.BlockSpec((nsub, N_STATE), lambda c: (c, 0))
        state_in = pl.BlockSpec((None, nsub, N_STATE), lambda c: (layer, c, 0))
        in_specs += [state_in, state_in]
        args += [h0_re, h0_im]
    state_shape = jax.ShapeDtypeStruct((seqs, N_STATE), F32)
    return pl.pallas_call(
        functools.partial(_s5_body, lt=lt, nsub=nsub, prompt=prompt),
        grid=(steps,),
        in_specs=in_specs,
        out_specs=[pl.BlockSpec((SEQ_ROWS, SSM_WIDTH), lambda c: (c, 0)), state_spec, state_spec],
        out_shape=[jax.ShapeDtypeStruct((m, SSM_WIDTH), BF16), state_shape, state_shape],
        scratch_shapes=[pltpu.VMEM((SEQ_ROWS, SSM_WIDTH), F32)],
        compiler_params=_params(("arbitrary",)),
        name="s5_prompt" if prompt else "s5_sample",
    )(*args)


def _gla_body(*refs, lt, nsub, prompt):
    if prompt:
        (x_ref, q_ref, k_ref, v_ref, gz_ref, wal_ref, bal_ref, wa2_ref, ba_ref, ng_ref, tri_ref,
         o_ref, s_ref, o_scr) = refs
        s0_ref = s_ref

        @pl.when(pl.program_id(0) == 0)
        def _():
            s_ref[...] = jnp.zeros_like(s_ref)
    else:
        (x_ref, q_ref, k_ref, v_ref, gz_ref, wal_ref, bal_ref, wa2_ref, ba_ref, ng_ref, tri_ref,
         s0_ref, _, o_ref, s_ref, o_scr) = refs
    tri = tri_ref[...]
    a_low = lax.dot_general(x_ref[...], wal_ref[...], _NT, preferred_element_type=F32) + bal_ref[...]
    logit = jnp.dot(a_low.astype(BF16), wa2_ref[...], preferred_element_type=F32) + ba_ref[...]
    log_a = jax.nn.log_sigmoid(logit) * (1.0 / GLA_TAU)
    hi = log_a.astype(BF16)
    lo = (log_a - hi.astype(F32)).astype(BF16)
    cum = (jnp.dot(tri, hi, preferred_element_type=F32) + jnp.dot(tri, lo, preferred_element_type=F32))
    q = q_ref[...].astype(F32) * (GLA_DK ** -0.5)
    k = k_ref[...].astype(F32)
    qd = (q * jnp.exp(cum)).astype(BF16)
    kd = (k * jnp.exp(-cum)).astype(BF16)
    cum3 = cum.reshape(nsub, lt, GLA_KEY_WIDTH)
    last = cum3[:, lt - 1:lt, :]
    kk = (k.reshape(nsub, lt, GLA_KEY_WIDTH) * jnp.exp(last - cum3)).astype(BF16)
    e_last = jnp.exp(last)
    ri = lax.broadcasted_iota(jnp.int32, (lt, lt), 0)
    ci = lax.broadcasted_iota(jnp.int32, (lt, lt), 1)
    causal = ri >= ci
    combos = [(s, h) for s in range(nsub) for h in range(GLA_HEADS)]
    rs = lambda s: slice(s * lt, (s + 1) * lt)
    ks = lambda h: slice(h * GLA_DK, (h + 1) * GLA_DK)
    vs = lambda h: slice(h * GLA_DV, (h + 1) * GLA_DV)
    atts = []
    for s, h in combos:
        att = lax.dot_general(qd[rs(s), ks(h)], kd[rs(s), ks(h)], _NT, preferred_element_type=F32)
        atts.append(jnp.where(causal, att, 0.0).astype(BF16))
    for (s, h), att in zip(combos, atts):
        o_scr[rs(s), vs(h)] = (jnp.dot(att, v_ref[rs(s), vs(h)], preferred_element_type=F32)
                               + jnp.dot(qd[rs(s), ks(h)], s0_ref[s, h].astype(BF16),
                                         preferred_element_type=F32))
    for s, h in combos:
        kv = lax.dot_general(kk[s, :, ks(h)], v_ref[rs(s), vs(h)], (((0,), (0,)), ((), ())),
                             preferred_element_type=F32)
        dec = jnp.transpose(jnp.broadcast_to(e_last[s, :, ks(h)], (GLA_DK, GLA_DK)))
        s_ref[s, h] = jnp.concatenate([dec, dec], axis=1) * s0_ref[s, h] + kv
    for h in range(GLA_HEADS):
        o = o_scr[:, vs(h)]
        mu = jnp.mean(o, axis=-1, keepdims=True)
        var = jnp.mean(jnp.square(o - mu), axis=-1, keepdims=True)
        o = (o - mu) * lax.rsqrt(var + LN_EPS) * ng_ref[:, vs(h)]
        o_ref[:, vs(h)] = (o * gz_ref[:, vs(h)].astype(F32)).astype(o_ref.dtype)


def _gla_branch(layer, xb, proj, wts, s0, s_stack, *, lt, prompt):
    m = proj.shape[0]
    rows = SEQ_ROWS if prompt else GLA_SAMPLE_ROWS
    nsub = rows // lt
    steps = m // rows
    tri = _block_tri(rows, lt)
    col = lambda off, w: pl.BlockSpec((rows, w), lambda c: (c, off // w))
    in_specs = [pl.BlockSpec((rows, D_MODEL), lambda c: (c, 0)),
                col(COL_Q, GLA_KEY_WIDTH), col(COL_K, GLA_KEY_WIDTH), col(COL_V, GLA_VAL_WIDTH),
                col(COL_GLA_Z, GLA_VAL_WIDTH),
                _layer_spec(layer, (ALOW_PAD, D_MODEL)), _layer_spec(layer, (1, ALOW_PAD)),
                _layer_spec(layer, (ALOW_PAD, GLA_KEY_WIDTH)), _layer_spec(layer, (1, GLA_KEY_WIDTH)),
                _layer_spec(layer, (1, GLA_VAL_WIDTH)), _const_spec((rows, rows))]
    args = [xb, proj, proj, proj, proj, wts['w_alow'], wts['b_alow'], wts['gla_w_a2'], wts['gla_b_a'],
            wts['gla_norm_g'], tri]
    state_dims = (GLA_HEADS, GLA_DK, GLA_DV)
    aliases = {}
    if prompt:
        state_spec = pl.BlockSpec((nsub,) + state_dims, lambda c: (0, 0, 0, 0))
        state_shape = jax.ShapeDtypeStruct((nsub,) + state_dims, F32)
    else:
        state_spec = pl.BlockSpec((None, nsub) + state_dims, lambda c: (layer, c, 0, 0, 0))
        state_shape = jax.ShapeDtypeStruct(s0.shape, F32)
        in_specs += [state_spec, pl.BlockSpec(memory_space=pl.ANY)]
        args += [s0, s_stack]
        aliases = {len(args) - 1: 1}
    return pl.pallas_call(
        functools.partial(_gla_body, lt=lt, nsub=nsub, prompt=prompt),
        grid=(steps,),
        in_specs=in_specs,
        out_specs=[pl.BlockSpec((rows, GLA_VAL_WIDTH), lambda c: (c, 0)), state_spec],
        out_shape=[jax.ShapeDtypeStruct((m, GLA_VAL_WIDTH), BF16), state_shape],
        scratch_shapes=[pltpu.VMEM((rows, GLA_VAL_WIDTH), F32)],
        input_output_aliases=aliases,
        compiler_params=_params(("arbitrary",)),
        name="gla_prompt" if prompt else "gla_sample",
    )(*args)


def _conv_epilogue(y, b_ref, lng_ref, lnb_ref, cz_ref, o_ref):
    y = y + b_ref[...]
    mu = jnp.mean(y, axis=-1, keepdims=True)
    var = jnp.mean(jnp.square(y - mu), axis=-1, keepdims=True)
    y = (y - mu) * lax.rsqrt(var + LN_EPS) * lng_ref[...] + lnb_ref[...]
    o_ref[...] = (jax.nn.silu(y) * cz_ref[...].astype(F32)).astype(o_ref.dtype)


def _conv_sample_body(ca_ref, scb_ref, cz_ref, w_ref, b_ref, lng_ref, lnb_ref, cache_ref, _,
                      o_ref, cache_out_ref, x_scr, y_scr, *, lt, nsub):
    pad = CONV_HIST - (CONV_K - 1)
    taps_by_shift = [[j for j in range(CONV_K) if (j + pad) % SUBLANES == sh] for sh in range(SUBLANES)]

    def per_seq(s, carry):
        r0 = pl.multiple_of(s * lt, SUBLANES)
        x_scr[s, CONV_HIST:CONV_HIST + lt, :] = (ca_ref[pl.ds(r0, lt), :].astype(F32)
                                                 * scb_ref[pl.ds(r0, lt), :].astype(F32))
        x_scr[s, pad:CONV_HIST, :] = cache_ref[s]
        acc = None
        for sh, taps in enumerate(taps_by_shift):
            span = max(taps) + pad - sh + lt
            xs = x_scr[s, pl.ds(sh, span), :]
            part = None
            for j in taps:
                base = j + pad - sh
                term = w_ref[j:j + 1, :] * xs[base:base + lt, :]
                part = term if part is None else part + term
            acc = part if acc is None else acc + part
        y_scr[pl.ds(r0, lt), :] = acc
        cache_out_ref[s] = x_scr[s, lt + pad:lt + CONV_HIST, :]
        return carry

    lax.fori_loop(0, nsub, per_seq, 0)
    _conv_epilogue(y_scr[...], b_ref, lng_ref, lnb_ref, cz_ref, o_ref)


CONV_STRIDE = 4
CONV_GROUP = SUBLANES * CONV_STRIDE
N_LANE_TILE = CONV_WIDTH // LANES


def _conv_prompt_body(ca_ref, scb_ref, cz_ref, w_ref, b_ref, lng_ref, lnb_ref, o_ref, cache_out_ref,
                      x_scr, y_scr, wb_scr, *, nsub):
    lt = CHUNK
    pitch = CONV_HIST + lt
    pad = CONV_HIST - (CONV_K - 1)

    @pl.when(pl.program_id(0) == 0)
    def _():
        for s in range(nsub):
            x_scr[:, s * pitch:s * pitch + CONV_HIST, :] = jnp.zeros((N_LANE_TILE, CONV_HIST, LANES), F32)
        for j in range(CONV_K):
            wb_scr[j] = jnp.broadcast_to(w_ref[j:j + 1, :], (SUBLANES, CONV_WIDTH))

    def per_seq(s, carry):
        r0 = pl.multiple_of(s * lt, SUBLANES)
        x0 = pl.multiple_of(s * pitch, SUBLANES)
        g = ca_ref[pl.ds(r0, lt), :].astype(F32) * scb_ref[pl.ds(r0, lt), :].astype(F32)
        for l in range(N_LANE_TILE):
            x_scr[l, pl.ds(x0 + CONV_HIST, lt), :] = g[:, l * LANES:(l + 1) * LANES]
        for l in range(N_LANE_TILE):
            ls = slice(l * LANES, (l + 1) * LANES)
            outs = [grp + k for grp in range(0, lt, CONV_GROUP) for k in range(CONV_STRIDE)]
            accs = [[None, None] for _ in outs]
            for j in range(CONV_K):
                wj = wb_scr[j, :, ls]
                for i, t0 in enumerate(outs):
                    term = wj * x_scr[l, pl.ds(x0 + t0 + j + pad, SUBLANES, stride=CONV_STRIDE), :]
                    accs[i][j % 2] = term if accs[i][j % 2] is None else accs[i][j % 2] + term
            for i, t0 in enumerate(outs):
                y_scr[l, pl.ds(r0 + t0, SUBLANES, stride=CONV_STRIDE), :] = accs[i][0] + accs[i][1]
            cache_out_ref[s, :, ls] = x_scr[l, pl.ds(x0 + lt + pad, CONV_K - 1), :]
            x_scr[l, pl.ds(x0, CONV_HIST), :] = x_scr[l, pl.ds(x0 + lt, CONV_HIST), :]
        return carry

    lax.fori_loop(0, nsub, per_seq, 0)
    y = jnp.concatenate([y_scr[l] for l in range(N_LANE_TILE)], axis=1)
    _conv_epilogue(y, b_ref, lng_ref, lnb_ref, cz_ref, o_ref)


def _conv_branch(layer, proj, wts, cache, cache_stack, *, lt, prompt):
    m = proj.shape[0]
    nsub = SEQ_ROWS // lt
    steps = m // SEQ_ROWS
    col = lambda off: pl.BlockSpec((SEQ_ROWS, CONV_WIDTH), lambda c: (c, off // CONV_WIDTH))
    in_specs = [col(COL_CA), col(COL_CB), col(COL_CZ),
                _layer_spec(layer, (CONV_K, CONV_WIDTH)), _layer_spec(layer, (1, CONV_WIDTH)),
                _layer_spec(layer, (1, CONV_WIDTH)), _layer_spec(layer, (1, CONV_WIDTH))]
    args = [proj, proj, proj, wts['conv_w'], wts['conv_b'], wts['conv_ln_g'], wts['conv_ln_b']]
    aliases = {}
    if prompt:
        cache_spec = pl.BlockSpec((nsub, CONV_K - 1, CONV_WIDTH), lambda c: (0, 0, 0))
        cache_shape = jax.ShapeDtypeStruct((nsub, CONV_K - 1, CONV_WIDTH), F32)
    else:
        cache_spec = pl.BlockSpec((None, nsub, CONV_K - 1, CONV_WIDTH), lambda c: (layer, c, 0, 0))
        cache_shape = jax.ShapeDtypeStruct(cache.shape, F32)
        in_specs += [cache_spec, pl.BlockSpec(memory_space=pl.ANY)]
        args += [cache, cache_stack]
        aliases = {len(args) - 1: 1}
    if prompt:
        body = functools.partial(_conv_prompt_body, nsub=nsub)
        scratch = [pltpu.VMEM((N_LANE_TILE, nsub * (CONV_HIST + lt), LANES), F32),
                   pltpu.VMEM((N_LANE_TILE, SEQ_ROWS, LANES), F32),
                   pltpu.VMEM((CONV_K, SUBLANES, CONV_WIDTH), F32)]
    else:
        body = functools.partial(_conv_sample_body, lt=lt, nsub=nsub)
        scratch = [pltpu.VMEM((nsub, CONV_HIST + lt, CONV_WIDTH), F32), pltpu.VMEM((SEQ_ROWS, CONV_WIDTH), F32)]
    return pl.pallas_call(
        body,
        grid=(steps,),
        in_specs=in_specs,
        out_specs=[pl.BlockSpec((SEQ_ROWS, CONV_WIDTH), lambda c: (c, 0)), cache_spec],
        out_shape=[jax.ShapeDtypeStruct((m, CONV_WIDTH), BF16), cache_shape],
        scratch_shapes=scratch,
        input_output_aliases=aliases,
        compiler_params=_params(("arbitrary",)),
        name="conv_prompt" if prompt else "conv_sample",
    )(*args)


def _merge_body(a1_ref, a2_ref, a3_ref, g1_ref, g2_ref, g3_ref, p1_ref, p2_ref, p3_ref, o_ref):
    m = g1_ref[...].astype(F32) * jnp.dot(a1_ref[...], p1_ref[...], preferred_element_type=F32)
    m = m + g2_ref[...].astype(F32) * jnp.dot(a2_ref[...], p2_ref[...], preferred_element_type=F32)
    m = m + g3_ref[...].astype(F32) * jnp.dot(a3_ref[...], p3_ref[...], preferred_element_type=F32)
    o_ref[...] = m.astype(o_ref.dtype)


def _merge(layer, a_s5, a_gla, a_conv, proj, wts):
    m = a_s5.shape[0]
    rows = min(MERGE_ROWS, m)
    act = pl.BlockSpec((rows, SSM_WIDTH), lambda i: (i, 0))
    gate = lambda n: pl.BlockSpec((rows, D_MODEL), lambda i: (i, COL_GATES // D_MODEL + n))
    wspec = _layer_spec(layer, (SSM_WIDTH, D_MODEL))
    return pl.pallas_call(
        _merge_body,
        grid=(m // rows,),
        in_specs=[act, act, act, gate(0), gate(1), gate(2), wspec, wspec, wspec],
        out_specs=pl.BlockSpec((rows, D_MODEL), lambda i: (i, 0)),
        out_shape=jax.ShapeDtypeStruct((m, D_MODEL), BF16),
        compiler_params=_params(("arbitrary",)),
        name="merge",
    )(a_s5, a_gla, a_conv, proj, proj, proj, wts['p_s5'], wts['p_gla'], wts['p_conv'])


def _post_body(m_ref, x_ref, p_ref, wo_ref, wpg_ref, wpe_ref, g_ref, b_ref, xo_ref, xb_ref):
    def piece(ref, i):
        if len(ref.shape) == 3:
            return ref[:, i * CHUNK:(i + 1) * CHUNK, :].reshape(SEQ_ROWS, ref.shape[-1])
        return ref[i * SEQ_ROWS:(i + 1) * SEQ_ROWS, :]

    for i in range(m_ref.shape[0] // SEQ_ROWS):
        h = DN_ALPHA * piece(x_ref, i) + jnp.dot(piece(m_ref, i), wo_ref[...], preferred_element_type=F32)
        gate = jax.nn.sigmoid(jnp.dot(h.astype(BF16), wpg_ref[...], preferred_element_type=F32))
        h = h + gate * jnp.dot(piece(p_ref, i).astype(BF16), wpe_ref[...], preferred_element_type=F32)
        mu = jnp.mean(h, axis=-1, keepdims=True)
        var = jnp.mean(jnp.square(h - mu), axis=-1, keepdims=True)
        xn = (h - mu) * lax.rsqrt(var + LN_EPS) * g_ref[...] + b_ref[...]
        if len(xo_ref.shape) == 3:
            xo_ref[:, i * CHUNK:(i + 1) * CHUNK, :] = xn.reshape(xo_ref.shape[0], CHUNK, D_MODEL)
        else:
            xo_ref[i * SEQ_ROWS:(i + 1) * SEQ_ROWS, :] = xn
        xb_ref[i * SEQ_ROWS:(i + 1) * SEQ_ROWS, :] = xn.astype(BF16)


def _post(layer, merged, x, p, wts, *, prompt, batch_major_out=False):
    m = merged.shape[0]
    rows = min(MERGE_ROWS, m)
    row = lambda w: pl.BlockSpec((rows, w), lambda i: (i, 0))
    if prompt:
        bp = p.shape[1]
        p_spec = pl.BlockSpec((None, bp, rows // bp, PLE_DIM), lambda i: (layer, 0, i, 0))
        bm_spec = pl.BlockSpec((bp, rows // bp, D_MODEL), lambda i: (0, i, 0))
    else:
        p_spec = pl.BlockSpec((None, rows, PLE_DIM), lambda i: (layer, i, 0))
    x_spec = bm_spec if x.ndim == 3 else row(D_MODEL)
    if batch_major_out:
        xo_spec, xo_shape = bm_spec, jax.ShapeDtypeStruct((bp, m // bp, D_MODEL), F32)
    else:
        xo_spec, xo_shape = row(D_MODEL), jax.ShapeDtypeStruct((m, D_MODEL), F32)
    return pl.pallas_call(
        _post_body,
        grid=(m // rows,),
        in_specs=[row(D_MODEL), x_spec, p_spec,
                  _layer_spec(layer, (D_MODEL, D_MODEL)), _layer_spec(layer, (D_MODEL, D_MODEL)),
                  _layer_spec(layer, (PLE_DIM, D_MODEL)),
                  _layer_spec(layer, (1, D_MODEL)), _layer_spec(layer, (1, D_MODEL))],
        out_specs=[xo_spec, row(D_MODEL)],
        out_shape=[xo_shape, jax.ShapeDtypeStruct((m, D_MODEL), BF16)],
        compiler_params=_params(("arbitrary",)),
        name="post",
    )(merged, x, p, wts['w_o'], wts['w_pg'], wts['w_pe'], wts['ln_g'], wts['ln_b'])


def _layer(layer, x, xb, p, wts, states, stacks, *, lt, prompt, batch_major_out=False):
    s5_re, s5_im, gla_s, conv_buf = states
    gla_stack, conv_stack = stacks
    proj = _inproj(layer, xb, wts)
    a_s5, h_re, h_im = _s5_branch(layer, proj, wts, s5_re, s5_im, lt=lt, prompt=prompt)
    a_gla, gla_new = _gla_branch(layer, xb, proj, wts, gla_s, gla_stack, lt=lt, prompt=prompt)
    a_conv, conv_new = _conv_branch(layer, proj, wts, conv_buf, conv_stack, lt=lt, prompt=prompt)
    merged = _merge(layer, a_s5, a_gla, a_conv, proj, wts)
    x_new, xb_new = _post(layer, merged, x, p, wts, prompt=prompt, batch_major_out=batch_major_out)
    return x_new, xb_new, h_re, h_im, gla_new, conv_new


def _to_chunk_major(a, bsz, seq):
    w = a.shape[-1]
    a = a.reshape(bsz, seq // CHUNK, CHUNK, w)
    return jnp.transpose(a, (1, 0, 2, 3)).reshape(bsz * seq, w)


def kernel(x_prompt, x_sample, p_prompt, p_sample, state_s5_re, state_s5_im, state_gla, cache_conv, w_in, b_in, s5_a_re, s5_a_im, s5_log_dt, s5_b_re, s5_b_im, s5_c_re, s5_c_im, s5_d, w_glu, b_glu, gla_w_a2, gla_b_a, gla_norm_g, conv_w, conv_b, conv_ln_g, conv_ln_b, p_s5, p_gla, p_conv, w_o, w_pg, w_pe, ln_g, ln_b):
    bp, seq, _ = x_prompt.shape
    bs, dseq, _ = x_sample.shape
    depth = w_in.shape[0]
    assert seq % CHUNK == 0 and bp * CHUNK == SEQ_ROWS and (bs * dseq) % SEQ_ROWS == 0 and dseq == SUBLANES

    apr, api, anr, ani, bbr, bbi = _s5_prep(s5_a_re, s5_a_im, s5_log_dt, s5_b_re, s5_b_im)
    s5_bre, s5_bim = _s5_block_diag(bbr), _s5_block_diag(bbi)
    s5_cre, s5_cim = _s5_block_diag(_s5_out_map(s5_c_re)), _s5_block_diag(_s5_out_map(s5_c_im))
    row = lambda a: a[:, None, :]
    alow = slice(*_IN_SRC['alow'])
    pad_rank = (0, ALOW_PAD - GLA_GATE_RANK)
    w_in_t = jnp.transpose(w_in, (0, 2, 1))
    wts = dict(
        w_in=w_in_t, in_table=_in_block_table(), b_in=_reorder_bias(b_in),
        w_alow=jnp.pad(w_in_t[:, alow, :], ((0, 0), pad_rank, (0, 0))).astype(BF16),
        b_alow=row(jnp.pad(b_in[:, alow], ((0, 0), pad_rank))),
        s5_bre=s5_bre, s5_bim=s5_bim, s5_cre=s5_cre, s5_cim=s5_cim,
        s5_apr=apr, s5_api=api, s5_anr=anr, s5_ani=ani,
        s5_d=row(s5_d), w_glu=w_glu.astype(BF16), b_glu=row(b_glu),
        gla_w_a2=jnp.pad(gla_w_a2, ((0, 0), pad_rank, (0, 0))).astype(BF16),
        gla_b_a=row(gla_b_a), gla_norm_g=row(gla_norm_g),
        conv_w=conv_w, conv_b=row(conv_b), conv_ln_g=row(conv_ln_g), conv_ln_b=row(conv_ln_b),
        p_s5=p_s5.astype(BF16), p_gla=p_gla.astype(BF16), p_conv=p_conv.astype(BF16),
        w_o=w_o.astype(BF16), w_pg=w_pg.astype(BF16), w_pe=w_pe.astype(BF16),
        ln_g=row(ln_g), ln_b=row(ln_b))

    hp, hpb = x_prompt, _to_chunk_major(x_prompt.astype(BF16), bp, seq)
    hs = x_sample.reshape(bs * dseq, D_MODEL)
    hsb = hs.astype(BF16)
    ps = p_sample.reshape(depth, bs * dseq, PLE_DIM)
    sample_states = (state_s5_re.reshape(depth, bs, N_STATE), state_s5_im.reshape(depth, bs, N_STATE),
                     state_gla, cache_conv)
    pr_re, pr_im, pr_gla, pr_conv, sm_re, sm_im = [], [], [], [], [], []
    gla_stack, conv_stack = jnp.zeros(state_gla.shape, F32), jnp.zeros(cache_conv.shape, F32)
    for i in range(depth):
        hp, hpb, a, b, c, d = _layer(i, hp, hpb, p_prompt, wts, (None,) * 4, (None, None), lt=CHUNK, prompt=True,
                                     batch_major_out=i == depth - 1)
        pr_re.append(a); pr_im.append(b); pr_gla.append(c); pr_conv.append(d)
        hs, hsb, a, b, gla_stack, conv_stack = _layer(i, hs, hsb, ps, wts, sample_states, (gla_stack, conv_stack),
                                                      lt=dseq, prompt=False)
        sm_re.append(a); sm_im.append(b)
    s5_shape = lambda a, n: jnp.stack(a).reshape(depth, n, SSM_GROUPS, SSM_STATE)
    return (hp, hs.reshape(bs, dseq, D_MODEL),
            s5_shape(pr_re, bp), s5_shape(pr_im, bp), jnp.stack(pr_gla), jnp.stack(pr_conv),
            s5_shape(sm_re, bs), s5_shape(sm_im, bs), gla_stack, conv_stack)
```

```python
import functools

import numpy as np
import jax
import jax.numpy as jnp
from jax import lax
from jax.experimental import pallas as pl
from jax.experimental.pallas import tpu as pltpu

F32 = jnp.float32
BF16 = jnp.bfloat16

D_MODEL = 2048
PLE_DIM = 256
SSM_WIDTH = 1024
SSM_GROUP = 16
SSM_GROUPS = 64
SSM_STATE = 64
N_STATE = SSM_GROUPS * SSM_STATE
GLA_HEADS = 4
GLA_DK = 128
GLA_DV = 256
GLA_KEY_WIDTH = 512
GLA_VAL_WIDTH = 1024
GLA_GATE_RANK = 16
GLA_TAU = 16.0
CONV_WIDTH = 1024
CONV_K = 31
N_BRANCH = 3
DN_ALPHA = 8.0 ** 0.25
LN_EPS = 1e-5

SUBLANES = 8
LANES = 128
CONV_HIST = 32
CHUNK = 64
SEQ_ROWS = 256
GLA_SAMPLE_ROWS = 64
S5_SLAB = 1024
S5_USLAB = 256
N_SLAB = N_STATE // S5_SLAB
MERGE_ROWS = 512

VMEM_LIMIT = 56 * 1024 * 1024

PROJ_BN = 512
PROJ_BM = 4096
PROJ_WIDE_MAX_ROWS = 1024
PROJ_SUB = 512
ALOW_PAD = LANES

_IN_SIZES = [SSM_WIDTH, SSM_WIDTH, GLA_KEY_WIDTH, GLA_KEY_WIDTH, GLA_VAL_WIDTH, GLA_GATE_RANK, GLA_VAL_WIDTH,
             CONV_WIDTH, CONV_WIDTH, CONV_WIDTH, N_BRANCH * D_MODEL]
_IN_NAMES = ['s5_u', 's5_z', 'q', 'k', 'v', 'alow', 'gla_z', 'ca', 'cb', 'cz', 'gates']
_IN_OFFS = np.concatenate([[0], np.cumsum(_IN_SIZES)])
_IN_SRC = {n: (int(_IN_OFFS[i]), int(_IN_OFFS[i + 1])) for i, n in enumerate(_IN_NAMES)}
_IN_ORDER = ['gates', 's5_u', 'v', 'ca', 's5_z', 'gla_z', 'cz', 'cb', 'q', 'k']
_COL = dict(zip(_IN_ORDER, np.concatenate([[0], np.cumsum([_IN_SRC[n][1] - _IN_SRC[n][0]
                                                             for n in _IN_ORDER])]).tolist()))
COL_GATES, COL_S5_U, COL_V, COL_CA = _COL['gates'], _COL['s5_u'], _COL['v'], _COL['ca']
COL_S5_Z, COL_GLA_Z, COL_CZ, COL_CB = _COL['s5_z'], _COL['gla_z'], _COL['cz'], _COL['cb']
COL_Q, COL_K = _COL['q'], _COL['k']
PROJ_COLS = COL_K + GLA_KEY_WIDTH


def _layer_spec(layer, shape):
    zeros = (0,) * len(shape)
    return pl.BlockSpec((None,) + tuple(shape), lambda *_: (layer,) + zeros, pipeline_mode=pl.Buffered(1))


def _const_spec(shape):
    zeros = (0,) * len(shape)
    return pl.BlockSpec(tuple(shape), lambda *_: zeros, pipeline_mode=pl.Buffered(1))


def _params(sem):
    return pltpu.CompilerParams(dimension_semantics=sem, vmem_limit_bytes=VMEM_LIMIT)


def _in_block_table(bn):
    src = np.empty(PROJ_COLS, np.int64)
    for name in _IN_ORDER:
        lo, hi = _IN_SRC[name]
        src[_COL[name]:_COL[name] + hi - lo] = np.arange(lo, hi)
    blocks = src.reshape(-1, bn)
    assert (np.diff(blocks, axis=1) == 1).all() and (blocks[:, 0] % GLA_GATE_RANK == 0).all()
    return jnp.asarray(blocks[:, 0] // GLA_GATE_RANK, jnp.int32)


def _reorder_bias(b_in):
    parts = [b_in[:, _IN_SRC[n][0]:_IN_SRC[n][1]] for n in _IN_ORDER]
    return jnp.concatenate(parts, axis=-1)[:, None, :]


_NT = (((1,), (1,)), ((), ()))


def _inproj_body(tbl_ref, x_ref, w_ref, b_ref, o_ref, w_scr):
    j = pl.program_id(1)
    bn = o_ref.shape[1]
    is_sig = (j < COL_S5_U // bn) | ((j >= COL_CB // bn) & (j < COL_Q // bn))
    is_silu = (j >= COL_S5_Z // bn) & (j < COL_CB // bn)
    w_scr[...] = w_ref[0].astype(BF16)

    def run(act):
        for r in range(0, x_ref.shape[0], PROJ_SUB):
            acc = lax.dot_general(x_ref[r:r + PROJ_SUB, :], w_scr[...], _NT, preferred_element_type=F32)
            o_ref[r:r + PROJ_SUB, :] = act(acc + b_ref[...]).astype(o_ref.dtype)

    pl.when(is_sig)(lambda: run(jax.nn.sigmoid))
    pl.when(is_silu)(lambda: run(jax.nn.silu))
    pl.when(jnp.logical_not(is_sig | is_silu))(lambda: run(lambda a: a))


def _inproj(layer, x, wts):
    m = x.shape[0]
    bm = min(m, PROJ_BM)
    bn = PROJ_BN if bm > PROJ_WIDE_MAX_ROWS else 2 * PROJ_BN
    grid_spec = pltpu.PrefetchScalarGridSpec(
        num_scalar_prefetch=1,
        grid=(m // bm, PROJ_COLS // bn),
        in_specs=[
            pl.BlockSpec((bm, D_MODEL), lambda i, j, tbl: (i, 0), pipeline_mode=pl.Buffered(1)),
            pl.BlockSpec((pl.Element(1), pl.Element(bn), pl.Element(D_MODEL)),
                         lambda i, j, tbl: (layer, tbl[j] * GLA_GATE_RANK, 0)),
            pl.BlockSpec((None, 1, bn), lambda i, j, tbl: (layer, 0, j))],
        out_specs=pl.BlockSpec((bm, bn), lambda i, j, tbl: (i, j)),
        scratch_shapes=[pltpu.VMEM((bn, D_MODEL), BF16)])
    return pl.pallas_call(
        _inproj_body,
        grid_spec=grid_spec,
        out_shape=jax.ShapeDtypeStruct((m, PROJ_COLS), BF16),
        compiler_params=_params(("arbitrary", "arbitrary")),
        name="inproj",
    )(wts['in_table'][bn], x, wts['w_in'], wts['b_in'])


def _s5_prep_body(ar_ref, ai_ref, ldt_ref, bre_ref, bim_ref, apr_ref, api_ref, anr_ref, ani_ref, bbr_ref, bbi_ref):
    ar = ar_ref[0]
    ai = ai_ref[0]
    dt = jnp.exp(ldt_ref[0])
    xr = ar * dt
    xi = ai * dt
    t = lax.broadcasted_iota(jnp.int32, (CHUNK, N_STATE), 0).astype(F32)
    ang = t * xi
    cs = jnp.cos(ang)
    sn = jnp.sin(ang)
    mag = jnp.exp(t * xr)
    imag = jnp.exp(-(t * xr))
    apr_ref[0] = mag * cs
    api_ref[0] = mag * sn
    anr_ref[0] = imag * cs
    ani_ref[0] = -(imag * sn)
    e1 = jnp.exp(xr)
    nr = e1 * jnp.cos(xi) - 1.0
    ni = e1 * jnp.sin(xi)
    den = ar * ar + ai * ai
    cr = (nr * ar + ni * ai) / den
    ci = (ni * ar - nr * ai) / den
    bre = bre_ref[0]
    bim = bim_ref[0]
    bbr_ref[0] = cr * bre - ci * bim
    bbi_ref[0] = cr * bim + ci * bre


def _s5_prep(a_re, a_im, log_dt, b_re, b_im):
    depth = a_re.shape[0]
    ar = a_re.reshape(depth, 1, N_STATE)
    ai = a_im.reshape(depth, 1, N_STATE)
    ldt = jnp.broadcast_to(log_dt[:, :, None], (depth, SSM_GROUPS, SSM_STATE)).reshape(depth, 1, N_STATE)
    bre = jnp.transpose(b_re, (0, 3, 1, 2)).reshape(depth, SSM_GROUP, N_STATE)
    bim = jnp.transpose(b_im, (0, 3, 1, 2)).reshape(depth, SSM_GROUP, N_STATE)
    row = pl.BlockSpec((1, 1, N_STATE), lambda i: (i, 0, 0))
    grp = pl.BlockSpec((1, SSM_GROUP, N_STATE), lambda i: (i, 0, 0))
    tab = pl.BlockSpec((1, CHUNK, N_STATE), lambda i: (i, 0, 0))
    tab_shape = jax.ShapeDtypeStruct((depth, CHUNK, N_STATE), F32)
    grp_shape = jax.ShapeDtypeStruct((depth, SSM_GROUP, N_STATE), F32)
    return pl.pallas_call(
        _s5_prep_body,
        grid=(depth,),
        in_specs=[row, row, row, grp, grp],
        out_specs=[tab, tab, tab, tab, grp, grp],
        out_shape=[tab_shape, tab_shape, tab_shape, tab_shape, grp_shape, grp_shape],
        compiler_params=_params(("arbitrary",)),
        name="s5_prep",
    )(ar, ai, ldt, bre, bim)


def _s5_block_diag(a):
    depth = a.shape[0]
    a4 = jnp.transpose(a.reshape(depth, SSM_GROUP, N_SLAB, S5_SLAB), (0, 2, 1, 3))
    tiled = jnp.tile(a4, (1, 1, S5_USLAB // SSM_GROUP, 1))
    r = np.arange(S5_USLAB)[:, None] // SSM_GROUP
    c = np.arange(S5_SLAB)[None, :] // SSM_STATE
    return jnp.where(jnp.asarray(r == c), tiled, 0.0).astype(BF16)


def _s5_out_map(c):
    return jnp.transpose(c, (0, 2, 1, 3)).reshape(c.shape[0], SSM_GROUP, N_STATE)


def _cmul(ar, ai, br, bi):
    return ar * br - ai * bi, ar * bi + ai * br


def _s5_body(*refs, lt, nsub, prompt):
    if prompt:
        (u_ref, sz_ref, bre_ref, bim_ref, cre_ref, cim_ref, tri_ref, apr_ref, api_ref, anr_ref, ani_ref,
         d_ref, wglu_ref, bglu_ref, o_ref, hre_ref, him_ref, y_scr) = refs
        pre_ref, pim_ref = hre_ref, him_ref

        @pl.when(pl.program_id(0) == 0)
        def _():
            hre_ref[...] = jnp.zeros_like(hre_ref)
            him_ref[...] = jnp.zeros_like(him_ref)
    else:
        (u_ref, sz_ref, bre_ref, bim_ref, cre_ref, cim_ref, tri_ref, apr_ref, api_ref, anr_ref, ani_ref,
         d_ref, wglu_ref, bglu_ref, pre_ref, pim_ref, o_ref, hre_ref, him_ref, y_scr) = refs
    rows = nsub * lt
    tri = tri_ref[...]
    for nt in range(N_SLAB):
        sl = slice(nt * S5_SLAB, (nt + 1) * S5_SLAB)
        u_nt = u_ref[:, nt * S5_USLAB:(nt + 1) * S5_USLAB]
        bu_re = jnp.dot(u_nt, bre_ref[nt], preferred_element_type=F32).reshape(nsub, lt, S5_SLAB)
        bu_im = jnp.dot(u_nt, bim_ref[nt], preferred_element_type=F32).reshape(nsub, lt, S5_SLAB)
        w_re, w_im = _cmul(bu_re, bu_im, anr_ref[0:lt, sl][None], ani_ref[0:lt, sl][None])
        w_re = w_re.reshape(rows, S5_SLAB).astype(BF16)
        w_im = w_im.reshape(rows, S5_SLAB).astype(BF16)
        acc_re = jnp.dot(tri, w_re, preferred_element_type=F32).reshape(nsub, lt, S5_SLAB)
        acc_im = jnp.dot(tri, w_im, preferred_element_type=F32).reshape(nsub, lt, S5_SLAB)
        c_re, c_im = _cmul(apr_ref[1:2, sl], api_ref[1:2, sl], pre_ref[:, sl], pim_ref[:, sl])
        acc_re = acc_re + c_re[:, None, :]
        acc_im = acc_im + c_im[:, None, :]
        h_re, h_im = _cmul(acc_re, acc_im, apr_ref[0:lt, sl][None], api_ref[0:lt, sl][None])
        hre_ref[:, sl] = h_re[:, lt - 1, :]
        him_ref[:, sl] = h_im[:, lt - 1, :]
        h_re = h_re.reshape(rows, S5_SLAB).astype(BF16)
        h_im = h_im.reshape(rows, S5_SLAB).astype(BF16)
        y_scr[:, nt * S5_USLAB:(nt + 1) * S5_USLAB] = (
            lax.dot_general(h_re, cre_ref[nt], _NT, preferred_element_type=F32)
            - lax.dot_general(h_im, cim_ref[nt], _NT, preferred_element_type=F32))
    y = y_scr[...] + d_ref[...] * u_ref[...].astype(F32)
    y = jax.nn.gelu(y)
    gate = jnp.dot(y.astype(BF16), wglu_ref[...], preferred_element_type=F32) + bglu_ref[...]
    y = y * jax.nn.sigmoid(gate)
    o_ref[...] = (y * sz_ref[...].astype(F32)).astype(o_ref.dtype)


def _block_tri(rows, lt):
    r = np.arange(rows)
    same = (r[:, None] // lt) == (r[None, :] // lt)
    return jnp.asarray(np.where(same & (r[:, None] >= r[None, :]), 1.0, 0.0), dtype=BF16)


def _s5_branch(layer, proj, wts, h0_re, h0_im, *, lt, prompt):
    m = proj.shape[0]
    nsub = SEQ_ROWS // lt
    steps = m // SEQ_ROWS
    seqs = nsub if prompt else m // lt
    tri = _block_tri(SEQ_ROWS, lt)
    col = lambda off, w: pl.BlockSpec((SEQ_ROWS, w), lambda c: (c, off // w))
    tab = _layer_spec(layer, (CHUNK, N_STATE))
    in_specs = [col(COL_S5_U, SSM_WIDTH), col(COL_S5_Z, SSM_WIDTH),
                _layer_spec(layer, (N_SLAB, S5_USLAB, S5_SLAB)), _layer_spec(layer, (N_SLAB, S5_USLAB, S5_SLAB)),
                _layer_spec(layer, (N_SLAB, S5_USLAB, S5_SLAB)), _layer_spec(layer, (N_SLAB, S5_USLAB, S5_SLAB)),
                _const_spec((SEQ_ROWS, SEQ_ROWS)), tab, tab, tab, tab,
                _layer_spec(layer, (1, SSM_WIDTH)), _layer_spec(layer, (SSM_WIDTH, SSM_WIDTH)),
                _layer_spec(layer, (1, SSM_WIDTH))]
    args = [proj, proj, wts['s5_bre'], wts['s5_bim'], wts['s5_cre'], wts['s5_cim'], tri,
            wts['s5_apr'], wts['s5_api'], wts['s5_anr'], wts['s5_ani'], wts['s5_d'], wts['w_glu'], wts['b_glu']]
    if prompt:
        state_spec = pl.BlockSpec((nsub, N_STATE), lambda c: (0, 0))
    else:
        state_spec = pl.BlockSpec((nsub, N_STATE), lambda c: (c, 0))
        state_in = pl.BlockSpec((None, nsub, N_STATE), lambda c: (layer, c, 0))
        in_specs += [state_in, state_in]
        args += [h0_re, h0_im]
    state_shape = jax.ShapeDtypeStruct((seqs, N_STATE), F32)
    return pl.pallas_call(
        functools.partial(_s5_body, lt=lt, nsub=nsub, prompt=prompt),
        grid=(steps,),
        in_specs=in_specs,
        out_specs=[pl.BlockSpec((SEQ_ROWS, SSM_WIDTH), lambda c: (c, 0)), state_spec, state_spec],
        out_shape=[jax.ShapeDtypeStruct((m, SSM_WIDTH), BF16), state_shape, state_shape],
        scratch_shapes=[pltpu.VMEM((SEQ_ROWS, SSM_WIDTH), F32)],
        compiler_params=_params(("arbitrary",)),
        name="s5_prompt" if prompt else "s5_sample",
    )(*args)


def _gla_body(*refs, lt, nsub, prompt):
    if prompt:
        (x_ref, q_ref, k_ref, v_ref, gz_ref, wal_ref, bal_ref, wa2_ref, ba_ref, ng_ref, tri_ref,
         o_ref, s_ref, o_scr) = refs
        s0_ref = s_ref

        @pl.when(pl.program_id(0) == 0)
        def _():
            s_ref[...] = jnp.zeros_like(s_ref)
    else:
        (x_ref, q_ref, k_ref, v_ref, gz_ref, wal_ref, bal_ref, wa2_ref, ba_ref, ng_ref, tri_ref,
         s0_ref, _, o_ref, s_ref, o_scr) = refs
    tri = tri_ref[...]
    a_low = lax.dot_general(x_ref[...], wal_ref[...], _NT, preferred_element_type=F32) + bal_ref[...]
    logit = jnp.dot(a_low.astype(BF16), wa2_ref[...], preferred_element_type=F32) + ba_ref[...]
    log_a = jax.nn.log_sigmoid(logit) * (1.0 / GLA_TAU)
    hi = log_a.astype(BF16)
    lo = (log_a - hi.astype(F32)).astype(BF16)
    cum = (jnp.dot(tri, hi, preferred_element_type=F32) + jnp.dot(tri, lo, preferred_element_type=F32))
    q = q_ref[...].astype(F32) * (GLA_DK ** -0.5)
    k = k_ref[...].astype(F32)
    qd = (q * jnp.exp(cum)).astype(BF16)
    kd = (k * jnp.exp(-cum)).astype(BF16)
    cum3 = cum.reshape(nsub, lt, GLA_KEY_WIDTH)
    last = cum3[:, lt - 1:lt, :]
    kk = (k.reshape(nsub, lt, GLA_KEY_WIDTH) * jnp.exp(last - cum3)).astype(BF16)
    e_last = jnp.exp(last)
    ri = lax.broadcasted_iota(jnp.int32, (lt, lt), 0)
    ci = lax.broadcasted_iota(jnp.int32, (lt, lt), 1)
    causal = ri >= ci
    combos = [(s, h) for s in range(nsub) for h in range(GLA_HEADS)]
    rs = lambda s: slice(s * lt, (s + 1) * lt)
    ks = lambda h: slice(h * GLA_DK, (h + 1) * GLA_DK)
    vs = lambda h: slice(h * GLA_DV, (h + 1) * GLA_DV)
    atts = []
    for s, h in combos:
        att = lax.dot_general(qd[rs(s), ks(h)], kd[rs(s), ks(h)], _NT, preferred_element_type=F32)
        atts.append(jnp.where(causal, att, 0.0).astype(BF16))
    for (s, h), att in zip(combos, atts):
        o_scr[rs(s), vs(h)] = (jnp.dot(att, v_ref[rs(s), vs(h)], preferred_element_type=F32)
                               + jnp.dot(qd[rs(s), ks(h)], s0_ref[s, h].astype(BF16),
                                         preferred_element_type=F32))
    for s, h in combos:
        kv = lax.dot_general(kk[s, :, ks(h)], v_ref[rs(s), vs(h)], (((0,), (0,)), ((), ())),
                             preferred_element_type=F32)
        dec = jnp.transpose(jnp.broadcast_to(e_last[s, :, ks(h)], (GLA_DK, GLA_DK)))
        s_ref[s, h] = jnp.concatenate([dec, dec], axis=1) * s0_ref[s, h] + kv
    for h in range(GLA_HEADS):
        o = o_scr[:, vs(h)]
        mu = jnp.mean(o, axis=-1, keepdims=True)
        var = jnp.mean(jnp.square(o - mu), axis=-1, keepdims=True)
        o = (o - mu) * lax.rsqrt(var + LN_EPS) * ng_ref[:, vs(h)]
        o_ref[:, vs(h)] = (o * gz_ref[:, vs(h)].astype(F32)).astype(o_ref.dtype)


def _gla_branch(layer, xb, proj, wts, s0, s_stack, *, lt, prompt):
    m = proj.shape[0]
    rows = SEQ_ROWS if prompt else GLA_SAMPLE_ROWS
    nsub = rows // lt
    steps = m // rows
    tri = _block_tri(rows, lt)
    col = lambda off, w: pl.BlockSpec((rows, w), lambda c: (c, off // w))
    in_specs = [pl.BlockSpec((rows, D_MODEL), lambda c: (c, 0)),
                col(COL_Q, GLA_KEY_WIDTH), col(COL_K, GLA_KEY_WIDTH), col(COL_V, GLA_VAL_WIDTH),
                col(COL_GLA_Z, GLA_VAL_WIDTH),
                _layer_spec(layer, (ALOW_PAD, D_MODEL)), _layer_spec(layer, (1, ALOW_PAD)),
                _layer_spec(layer, (ALOW_PAD, GLA_KEY_WIDTH)), _layer_spec(layer, (1, GLA_KEY_WIDTH)),
                _layer_spec(layer, (1, GLA_VAL_WIDTH)), _const_spec((rows, rows))]
    args = [xb, proj, proj, proj, proj, wts['w_alow'], wts['b_alow'], wts['gla_w_a2'], wts['gla_b_a'],
            wts['gla_norm_g'], tri]
    state_dims = (GLA_HEADS, GLA_DK, GLA_DV)
    aliases = {}
    if prompt:
        state_spec = pl.BlockSpec((nsub,) + state_dims, lambda c: (0, 0, 0, 0))
        state_shape = jax.ShapeDtypeStruct((nsub,) + state_dims, F32)
    else:
        state_spec = pl.BlockSpec((None, nsub) + state_dims, lambda c: (layer, c, 0, 0, 0))
        state_shape = jax.ShapeDtypeStruct(s0.shape, F32)
        in_specs += [state_spec, pl.BlockSpec(memory_space=pl.ANY)]
        args += [s0, s_stack]
        aliases = {len(args) - 1: 1}
    return pl.pallas_call(
        functools.partial(_gla_body, lt=lt, nsub=nsub, prompt=prompt),
        grid=(steps,),
        in_specs=in_specs,
        out_specs=[pl.BlockSpec((rows, GLA_VAL_WIDTH), lambda c: (c, 0)), state_spec],
        out_shape=[jax.ShapeDtypeStruct((m, GLA_VAL_WIDTH), BF16), state_shape],
        scratch_shapes=[pltpu.VMEM((rows, GLA_VAL_WIDTH), F32)],
        input_output_aliases=aliases,
        compiler_params=_params(("arbitrary",)),
        name="gla_prompt" if prompt else "gla_sample",
    )(*args)


def _conv_epilogue(y, b_ref, lng_ref, lnb_ref, cz_ref, o_ref):
    y = y + b_ref[...]
    mu = jnp.mean(y, axis=-1, keepdims=True)
    var = jnp.mean(jnp.square(y - mu), axis=-1, keepdims=True)
    y = (y - mu) * lax.rsqrt(var + LN_EPS) * lng_ref[...] + lnb_ref[...]
    o_ref[...] = (jax.nn.silu(y) * cz_ref[...].astype(F32)).astype(o_ref.dtype)


def _conv_sample_body(ca_ref, scb_ref, cz_ref, w_ref, b_ref, lng_ref, lnb_ref, cache_ref, _,
                      o_ref, cache_out_ref, x_scr, y_scr, *, lt, nsub):
    pad = CONV_HIST - (CONV_K - 1)
    taps_by_shift = [[j for j in range(CONV_K) if (j + pad) % SUBLANES == sh] for sh in range(SUBLANES)]

    def per_seq(s, carry):
        r0 = pl.multiple_of(s * lt, SUBLANES)
        x_scr[s, CONV_HIST:CONV_HIST + lt, :] = (ca_ref[pl.ds(r0, lt), :].astype(F32)
                                                 * scb_ref[pl.ds(r0, lt), :].astype(F32))
        x_scr[s, pad:CONV_HIST, :] = cache_ref[s]
        acc = None
        for sh, taps in enumerate(taps_by_shift):
            span = max(taps) + pad - sh + lt
            xs = x_scr[s, pl.ds(sh, span), :]
            part = None
            for j in taps:
                base = j + pad - sh
                term = w_ref[j:j + 1, :] * xs[base:base + lt, :]
                part = term if part is None else part + term
            acc = part if acc is None else acc + part
        y_scr[pl.ds(r0, lt), :] = acc
        cache_out_ref[s] = x_scr[s, lt + pad:lt + CONV_HIST, :]
        return carry

    lax.fori_loop(0, nsub, per_seq, 0)
    _conv_epilogue(y_scr[...], b_ref, lng_ref, lnb_ref, cz_ref, o_ref)


CONV_STRIDE = 4
CONV_GROUP = SUBLANES * CONV_STRIDE
N_LANE_TILE = CONV_WIDTH // LANES


def _conv_prompt_body(ca_ref, scb_ref, cz_ref, w_ref, b_ref, lng_ref, lnb_ref, o_ref, cache_out_ref,
                      x_scr, y_scr, wb_scr, *, nsub):
    lt = CHUNK
    pitch = CONV_HIST + lt
    pad = CONV_HIST - (CONV_K - 1)

    @pl.when(pl.program_id(0) == 0)
    def _():
        for s in range(nsub):
            x_scr[:, s * pitch:s * pitch + CONV_HIST, :] = jnp.zeros((N_LANE_TILE, CONV_HIST, LANES), F32)
        for j in range(CONV_K):
            wb_scr[j] = jnp.broadcast_to(w_ref[j:j + 1, :], (SUBLANES, CONV_WIDTH))

    def per_seq(s, carry):
        r0 = pl.multiple_of(s * lt, SUBLANES)
        x0 = pl.multiple_of(s * pitch, SUBLANES)
        g = ca_ref[pl.ds(r0, lt), :].astype(F32) * scb_ref[pl.ds(r0, lt), :].astype(F32)
        for l in range(N_LANE_TILE):
            x_scr[l, pl.ds(x0 + CONV_HIST, lt), :] = g[:, l * LANES:(l + 1) * LANES]
        for l in range(N_LANE_TILE):
            ls = slice(l * LANES, (l + 1) * LANES)
            outs = [grp + k for grp in range(0, lt, CONV_GROUP) for k in range(CONV_STRIDE)]
            accs = [[None, None] for _ in outs]
            for j in range(CONV_K):
                wj = wb_scr[j, :, ls]
                for i, t0 in enumerate(outs):
                    term = wj * x_scr[l, pl.ds(x0 + t0 + j + pad, SUBLANES, stride=CONV_STRIDE), :]
                    accs[i][j % 2] = term if accs[i][j % 2] is None else accs[i][j % 2] + term
            for i, t0 in enumerate(outs):
                y_scr[l, pl.ds(r0 + t0, SUBLANES, stride=CONV_STRIDE), :] = accs[i][0] + accs[i][1]
            cache_out_ref[s, :, ls] = x_scr[l, pl.ds(x0 + lt + pad, CONV_K - 1), :]
            x_scr[l, pl.ds(x0, CONV_HIST), :] = x_scr[l, pl.ds(x0 + lt, CONV_HIST), :]
        return carry

    lax.fori_loop(0, nsub, per_seq, 0)
    y = jnp.concatenate([y_scr[l] for l in range(N_LANE_TILE)], axis=1)
    _conv_epilogue(y, b_ref, lng_ref, lnb_ref, cz_ref, o_ref)


def _conv_branch(layer, proj, wts, cache, cache_stack, *, lt, prompt):
    m = proj.shape[0]
    nsub = SEQ_ROWS // lt
    steps = m // SEQ_ROWS
    col = lambda off: pl.BlockSpec((SEQ_ROWS, CONV_WIDTH), lambda c: (c, off // CONV_WIDTH))
    in_specs = [col(COL_CA), col(COL_CB), col(COL_CZ),
                _layer_spec(layer, (CONV_K, CONV_WIDTH)), _layer_spec(layer, (1, CONV_WIDTH)),
                _layer_spec(layer, (1, CONV_WIDTH)), _layer_spec(layer, (1, CONV_WIDTH))]
    args = [proj, proj, proj, wts['conv_w'], wts['conv_b'], wts['conv_ln_g'], wts['conv_ln_b']]
    aliases = {}
    if prompt:
        cache_spec = pl.BlockSpec((nsub, CONV_K - 1, CONV_WIDTH), lambda c: (0, 0, 0))
        cache_shape = jax.ShapeDtypeStruct((nsub, CONV_K - 1, CONV_WIDTH), F32)
    else:
        cache_spec = pl.BlockSpec((None, nsub, CONV_K - 1, CONV_WIDTH), lambda c: (layer, c, 0, 0))
        cache_shape = jax.ShapeDtypeStruct(cache.shape, F32)
        in_specs += [cache_spec, pl.BlockSpec(memory_space=pl.ANY)]
        args += [cache, cache_stack]
        aliases = {len(args) - 1: 1}
    if prompt:
        body = functools.partial(_conv_prompt_body, nsub=nsub)
        scratch = [pltpu.VMEM((N_LANE_TILE, nsub * (CONV_HIST + lt), LANES), F32),
                   pltpu.VMEM((N_LANE_TILE, SEQ_ROWS, LANES), F32),
                   pltpu.VMEM((CONV_K, SUBLANES, CONV_WIDTH), F32)]
    else:
        body = functools.partial(_conv_sample_body, lt=lt, nsub=nsub)
        scratch = [pltpu.VMEM((nsub, CONV_HIST + lt, CONV_WIDTH), F32), pltpu.VMEM((SEQ_ROWS, CONV_WIDTH), F32)]
    return pl.pallas_call(
        body,
        grid=(steps,),
        in_specs=in_specs,
        out_specs=[pl.BlockSpec((SEQ_ROWS, CONV_WIDTH), lambda c: (c, 0)), cache_spec],
        out_shape=[jax.ShapeDtypeStruct((m, CONV_WIDTH), BF16), cache_shape],
        scratch_shapes=scratch,
        input_output_aliases=aliases,
        compiler_params=_params(("arbitrary",)),
        name="conv_prompt" if prompt else "conv_sample",
    )(*args)


def _merge_body(a1_ref, a2_ref, a3_ref, g1_ref, g2_ref, g3_ref, p1_ref, p2_ref, p3_ref, o_ref):
    m = g1_ref[...].astype(F32) * jnp.dot(a1_ref[...], p1_ref[...], preferred_element_type=F32)
    m = m + g2_ref[...].astype(F32) * jnp.dot(a2_ref[...], p2_ref[...], preferred_element_type=F32)
    m = m + g3_ref[...].astype(F32) * jnp.dot(a3_ref[...], p3_ref[...], preferred_element_type=F32)
    o_ref[...] = m.astype(o_ref.dtype)


def _merge(layer, a_s5, a_gla, a_conv, proj, wts):
    m = a_s5.shape[0]
    rows = min(MERGE_ROWS, m)
    act = pl.BlockSpec((rows, SSM_WIDTH), lambda i: (i, 0))
    gate = lambda n: pl.BlockSpec((rows, D_MODEL), lambda i: (i, COL_GATES // D_MODEL + n))
    wspec = _layer_spec(layer, (SSM_WIDTH, D_MODEL))
    return pl.pallas_call(
        _merge_body,
        grid=(m // rows,),
        in_specs=[act, act, act, gate(0), gate(1), gate(2), wspec, wspec, wspec],
        out_specs=pl.BlockSpec((rows, D_MODEL), lambda i: (i, 0)),
        out_shape=jax.ShapeDtypeStruct((m, D_MODEL), BF16),
        compiler_params=_params(("arbitrary",)),
        name="merge",
    )(a_s5, a_gla, a_conv, proj, proj, proj, wts['p_s5'], wts['p_gla'], wts['p_conv'])


def _post_body(m_ref, x_ref, p_ref, wo_ref, wpg_ref, wpe_ref, g_ref, b_ref, xo_ref, xb_ref):
    def piece(ref, i):
        if len(ref.shape) == 3:
            return ref[:, i * CHUNK:(i + 1) * CHUNK, :].reshape(SEQ_ROWS, ref.shape[-1])
        return ref[i * SEQ_ROWS:(i + 1) * SEQ_ROWS, :]

    for i in range(m_ref.shape[0] // SEQ_ROWS):
        h = DN_ALPHA * piece(x_ref, i) + jnp.dot(piece(m_ref, i), wo_ref[...], preferred_element_type=F32)
        gate = jax.nn.sigmoid(jnp.dot(h.astype(BF16), wpg_ref[...], preferred_element_type=F32))
        h = h + gate * jnp.dot(piece(p_ref, i).astype(BF16), wpe_ref[...], preferred_element_type=F32)
        mu = jnp.mean(h, axis=-1, keepdims=True)
        var = jnp.mean(jnp.square(h - mu), axis=-1, keepdims=True)
        xn = (h - mu) * lax.rsqrt(var + LN_EPS) * g_ref[...] + b_ref[...]
        if len(xo_ref.shape) == 3:
            xo_ref[:, i * CHUNK:(i + 1) * CHUNK, :] = xn.reshape(xo_ref.shape[0], CHUNK, D_MODEL)
        else:
            xo_ref[i * SEQ_ROWS:(i + 1) * SEQ_ROWS, :] = xn
        xb_ref[i * SEQ_ROWS:(i + 1) * SEQ_ROWS, :] = xn.astype(BF16)


def _post(layer, merged, x, p, wts, *, prompt, batch_major_out=False):
    m = merged.shape[0]
    rows = min(MERGE_ROWS, m)
    row = lambda w: pl.BlockSpec((rows, w), lambda i: (i, 0))
    if prompt:
        bp = p.shape[1]
        p_spec = pl.BlockSpec((None, bp, rows // bp, PLE_DIM), lambda i: (layer, 0, i, 0))
        bm_spec = pl.BlockSpec((bp, rows // bp, D_MODEL), lambda i: (0, i, 0))
    else:
        p_spec = pl.BlockSpec((None, rows, PLE_DIM), lambda i: (layer, i, 0))
    x_spec = bm_spec if x.ndim == 3 else row(D_MODEL)
    if batch_major_out:
        xo_spec, xo_shape = bm_spec, jax.ShapeDtypeStruct((bp, m // bp, D_MODEL), F32)
    else:
        xo_spec, xo_shape = row(D_MODEL), jax.ShapeDtypeStruct((m, D_MODEL), F32)
    return pl.pallas_call(
        _post_body,
        grid=(m // rows,),
        in_specs=[row(D_MODEL), x_spec, p_spec,
                  _layer_spec(layer, (D_MODEL, D_MODEL)), _layer_spec(layer, (D_MODEL, D_MODEL)),
                  _layer_spec(layer, (PLE_DIM, D_MODEL)),
                  _layer_spec(layer, (1, D_MODEL)), _layer_spec(layer, (1, D_MODEL))],
        out_specs=[xo_spec, row(D_MODEL)],
        out_shape=[xo_shape, jax.ShapeDtypeStruct((m, D_MODEL), BF16)],
        compiler_params=_params(("arbitrary",)),
        name="post",
    )(merged, x, p, wts['w_o'], wts['w_pg'], wts['w_pe'], wts['ln_g'], wts['ln_b'])


def _layer(layer, x, xb, p, wts, states, stacks, *, lt, prompt, batch_major_out=False):
    s5_re, s5_im, gla_s, conv_buf = states
    gla_stack, conv_stack = stacks
    proj = _inproj(layer, xb, wts)
    a_s5, h_re, h_im = _s5_branch(layer, proj, wts, s5_re, s5_im, lt=lt, prompt=prompt)
    a_gla, gla_new = _gla_branch(layer, xb, proj, wts, gla_s, gla_stack, lt=lt, prompt=prompt)
    a_conv, conv_new = _conv_branch(layer, proj, wts, conv_buf, conv_stack, lt=lt, prompt=prompt)
    merged = _merge(layer, a_s5, a_gla, a_conv, proj, wts)
    x_new, xb_new = _post(layer, merged, x, p, wts, prompt=prompt, batch_major_out=batch_major_out)
    return x_new, xb_new, h_re, h_im, gla_new, conv_new


def _to_chunk_major(a, bsz, seq):
    w = a.shape[-1]
    a = a.reshape(bsz, seq // CHUNK, CHUNK, w)
    return jnp.transpose(a, (1, 0, 2, 3)).reshape(bsz * seq, w)


def kernel(x_prompt, x_sample, p_prompt, p_sample, state_s5_re, state_s5_im, state_gla, cache_conv, w_in, b_in, s5_a_re, s5_a_im, s5_log_dt, s5_b_re, s5_b_im, s5_c_re, s5_c_im, s5_d, w_glu, b_glu, gla_w_a2, gla_b_a, gla_norm_g, conv_w, conv_b, conv_ln_g, conv_ln_b, p_s5, p_gla, p_conv, w_o, w_pg, w_pe, ln_g, ln_b):
    bp, seq, _ = x_prompt.shape
    bs, dseq, _ = x_sample.shape
    depth = w_in.shape[0]
    assert seq % CHUNK == 0 and bp * CHUNK == SEQ_ROWS and (bs * dseq) % SEQ_ROWS == 0 and dseq == SUBLANES

    apr, api, anr, ani, bbr, bbi = _s5_prep(s5_a_re, s5_a_im, s5_log_dt, s5_b_re, s5_b_im)
    s5_bre, s5_bim = _s5_block_diag(bbr), _s5_block_diag(bbi)
    s5_cre, s5_cim = _s5_block_diag(_s5_out_map(s5_c_re)), _s5_block_diag(_s5_out_map(s5_c_im))
    row = lambda a: a[:, None, :]
    alow = slice(*_IN_SRC['alow'])
    pad_rank = (0, ALOW_PAD - GLA_GATE_RANK)
    w_in_t = jnp.transpose(w_in, (0, 2, 1))
    wts = dict(
        w_in=w_in_t, in_table={bn: _in_block_table(bn) for bn in (PROJ_BN, 2 * PROJ_BN)}, b_in=_reorder_bias(b_in),
        w_alow=jnp.pad(w_in_t[:, alow, :], ((0, 0), pad_rank, (0, 0))).astype(BF16),
        b_alow=row(jnp.pad(b_in[:, alow], ((0, 0), pad_rank))),
        s5_bre=s5_bre, s5_bim=s5_bim, s5_cre=s5_cre, s5_cim=s5_cim,
        s5_apr=apr, s5_api=api, s5_anr=anr, s5_ani=ani,
        s5_d=row(s5_d), w_glu=w_glu.astype(BF16), b_glu=row(b_glu),
        gla_w_a2=jnp.pad(gla_w_a2, ((0, 0), pad_rank, (0, 0))).astype(BF16),
        gla_b_a=row(gla_b_a), gla_norm_g=row(gla_norm_g),
        conv_w=conv_w, conv_b=row(conv_b), conv_ln_g=row(conv_ln_g), conv_ln_b=row(conv_ln_b),
        p_s5=p_s5.astype(BF16), p_gla=p_gla.astype(BF16), p_conv=p_conv.astype(BF16),
        w_o=w_o.astype(BF16), w_pg=w_pg.astype(BF16), w_pe=w_pe.astype(BF16),
        ln_g=row(ln_g), ln_b=row(ln_b))

    hp, hpb = x_prompt, _to_chunk_major(x_prompt.astype(BF16), bp, seq)
    hs = x_sample.reshape(bs * dseq, D_MODEL)
    hsb = hs.astype(BF16)
    ps = p_sample.reshape(depth, bs * dseq, PLE_DIM)
    sample_states = (state_s5_re.reshape(depth, bs, N_STATE), state_s5_im.reshape(depth, bs, N_STATE),
                     state_gla, cache_conv)
    pr_re, pr_im, pr_gla, pr_conv, sm_re, sm_im = [], [], [], [], [], []
    gla_stack, conv_stack = jnp.zeros(state_gla.shape, F32), jnp.zeros(cache_conv.shape, F32)
    for i in range(depth):
        hp, hpb, a, b, c, d = _layer(i, hp, hpb, p_prompt, wts, (None,) * 4, (None, None), lt=CHUNK, prompt=True,
                                     batch_major_out=i == depth - 1)
        pr_re.append(a); pr_im.append(b); pr_gla.append(c); pr_conv.append(d)
        hs, hsb, a, b, gla_stack, conv_stack = _layer(i, hs, hsb, ps, wts, sample_states, (gla_stack, conv_stack),
                                                      lt=dseq, prompt=False)
        sm_re.append(a); sm_im.append(b)
    s5_shape = lambda a, n: jnp.stack(a).reshape(depth, n, SSM_GROUPS, SSM_STATE)
    return (hp, hs.reshape(bs, dseq, D_MODEL),
            s5_shape(pr_re, bp), s5_shape(pr_im, bp), jnp.stack(pr_gla), jnp.stack(pr_conv),
            s5_shape(sm_re, bs), s5_shape(sm_im, bs), gla_stack, conv_stack)
```
